```python
import math
import jax, jax.numpy as jnp
from jax import lax
import numpy as np

D_MODEL = 1024
BATCH = 16
SEQ = 2048
DEPTH = 2

N_HEADS = 16
HEAD_DIM = D_MODEL // N_HEADS
N_KV_HEADS = 4
GROUP = N_HEADS // N_KV_HEADS
ATTN_WIDTH = N_HEADS * HEAD_DIM
KV_WIDTH = N_KV_HEADS * HEAD_DIM
IDX_HEADS = 8
IDX_DIM = 64
DSA_TOPK_MAX = 256
MOBA_BLOCK = 256
MOBA_TOPK = 3
REL_BUCKETS = 32
REL_MAX_DIST = 128
EPS = 1e-6
N_A_LAYERS = max(1, DEPTH // 2)
N_B_LAYERS = DEPTH - N_A_LAYERS
QBLK_A = 64
QBLK_B = 16
A_PARTS = [ATTN_WIDTH, KV_WIDTH, KV_WIDTH, ATTN_WIDTH, IDX_HEADS * IDX_DIM, IDX_HEADS, IDX_DIM]
A_COLS = sum(A_PARTS)
A_SPLITS = list(np.cumsum(A_PARTS)[:-1])

kernel_name = "yoco_dsa_moba_hybrid"


def rmsnorm(x, g):
    xf = x.astype(jnp.float32)
    y = xf * lax.rsqrt(jnp.mean(xf * xf, axis=-1, keepdims=True) + EPS)
    return (y * g.astype(jnp.float32)).astype(x.dtype)


def rel_bucket(dist):
    n = jnp.maximum(dist, 0)
    max_exact = REL_BUCKETS // 2
    nf = jnp.maximum(n, 1).astype(jnp.float32)
    large = max_exact + (jnp.log(nf / max_exact) / math.log(REL_MAX_DIST / max_exact)
                         * (REL_BUCKETS - max_exact)).astype(jnp.int32)
    large = jnp.minimum(large, REL_BUCKETS - 1)
    return jnp.where(n < max_exact, n, large)


def dsa_layer(x, norm_g, w_in, qn_g, kn_g, w_out, rel_bias):
    B, T, _ = x.shape
    h = rmsnorm(x, norm_g)
    q, k, v, gate, iq, iw, ik = jnp.split(h @ w_in, A_SPLITS, axis=-1)
    q = rmsnorm(q.reshape(B, T, N_KV_HEADS, GROUP, HEAD_DIM), qn_g)
    k = rmsnorm(k.reshape(B, T, N_KV_HEADS, HEAD_DIM), kn_g)
    v = v.reshape(B, T, N_KV_HEADS, HEAD_DIM)
    iq = iq.reshape(B, T, IDX_HEADS, IDX_DIM)
    iw = iw * IDX_HEADS ** -0.5
    topk = min(DSA_TOPK_MAX, T // 4)
    nblk = T // QBLK_A
    spos = jnp.arange(T)
    bidx = jnp.arange(B)[:, None, None]

    def block(i):
        t0 = i * QBLK_A
        qb = lax.dynamic_slice_in_dim(q, t0, QBLK_A, axis=1)
        iqb = lax.dynamic_slice_in_dim(iq, t0, QBLK_A, axis=1)
        iwb = lax.dynamic_slice_in_dim(iw, t0, QBLK_A, axis=1)
        tpos = t0 + jnp.arange(QBLK_A)
        causal = spos[None, :] <= tpos[:, None]
        isc = jnp.einsum('bqhd,bsd->bqhs', iqb, ik) * IDX_DIM ** -0.5
        iscore = jnp.einsum('bqh,bqhs->bqs', iwb, jax.nn.relu(isc)).astype(jnp.float32)
        iscore = jnp.where(causal[None], iscore, -jnp.inf)
        _, sel = lax.top_k(iscore, topk)
        valid = sel <= tpos[None, :, None]
        ks = k[bidx, sel]
        vs = v[bidx, sel]
        logits = jnp.einsum('bqgjd,bqkgd->bqgjk', qb, ks).astype(jnp.float32) * HEAD_DIM ** -0.5
        bias = rel_bias[rel_bucket(tpos[None, :, None] - sel)]
        bias = bias.reshape(B, QBLK_A, topk, N_KV_HEADS, GROUP).transpose(0, 1, 3, 4, 2)
        logits = jnp.where(valid[:, :, None, None, :], logits + bias.astype(jnp.float32), -jnp.inf)
        p = jax.nn.softmax(logits, axis=-1).astype(vs.dtype)
        o = jnp.einsum('bqgjk,bqkgd->bqgjd', p, vs)
        return o.reshape(B, QBLK_A, ATTN_WIDTH)

    o = lax.map(block, jnp.arange(nblk))
    o = o.transpose(1, 0, 2, 3).reshape(B, T, ATTN_WIDTH)
    return x + (o * jax.nn.silu(gate)) @ w_out


def shared_kv(x, norm_g, w_kv, kn_g):
    B, T, _ = x.shape
    h = rmsnorm(x, norm_g)
    k, v = jnp.split(h @ w_kv, 2, axis=-1)
    k = rmsnorm(k.reshape(B, T, N_KV_HEADS, HEAD_DIM), kn_g)
    v = v.reshape(B, T, N_KV_HEADS, HEAD_DIM)
    nb = -(-T // MOBA_BLOCK)
    pad = nb * MOBA_BLOCK - T
    kp = jnp.pad(k, ((0, 0), (0, pad), (0, 0), (0, 0)))
    vp = jnp.pad(v, ((0, 0), (0, pad), (0, 0), (0, 0)))
    kb = kp.reshape(B, nb, MOBA_BLOCK, N_KV_HEADS, HEAD_DIM).transpose(0, 3, 1, 2, 4)
    vb = vp.reshape(B, nb, MOBA_BLOCK, N_KV_HEADS, HEAD_DIM).transpose(0, 3, 1, 2, 4)
    counts = jnp.minimum(T - jnp.arange(nb) * MOBA_BLOCK, MOBA_BLOCK).astype(kb.dtype)
    kmean = kb.sum(axis=3) / counts[None, None, :, None]
    return kb, vb, kmean


def moba_layer(x, norm_g, w_in, qn_g, w_out, rel_bias, kb, vb, kmean):
    B, T, _ = x.shape
    h = rmsnorm(x, norm_g)
    q, gate = jnp.split(h @ w_in, 2, axis=-1)
    q = rmsnorm(q.reshape(B, T, N_KV_HEADS, GROUP, HEAD_DIM), qn_g)
    nb = kb.shape[2]
    nsel = min(MOBA_TOPK, nb - 1)
    bias_g = rel_bias.reshape(REL_BUCKETS, N_KV_HEADS, GROUP).transpose(1, 0, 2)
    r = jnp.arange(MOBA_BLOCK)
    bidx = jnp.arange(B)[:, None, None, None]
    gidx = jnp.arange(N_KV_HEADS)[None, None, :, None]
    nblk = T // QBLK_B

    def block(i):
        t0 = i * QBLK_B
        own = t0 // MOBA_BLOCK
        qb = lax.dynamic_slice_in_dim(q, t0, QBLK_B, axis=1)
        tpos = t0 + jnp.arange(QBLK_B)
        own_blk = jnp.full((B, QBLK_B, N_KV_HEADS, 1), own, dtype=jnp.int32)
        own_ok = jnp.ones((B, QBLK_B, N_KV_HEADS, 1), dtype=bool)
        if nsel > 0:
            gs = jnp.einsum('bqgjd,bgnd->bqgn', qb, kmean).astype(jnp.float32)
            gs = jnp.where(jnp.arange(nb) < own, gs, -jnp.inf)
            gv, sel = lax.top_k(gs, nsel)
            blocks = jnp.concatenate([sel.astype(jnp.int32), own_blk], axis=-1)
            ok = jnp.concatenate([jnp.isfinite(gv), own_ok], axis=-1)
        else:
            blocks, ok = own_blk, own_ok
        S = blocks.shape[-1]
        ks = kb[bidx, gidx, blocks]
        vs = vb[bidx, gidx, blocks]
        kpos = blocks[..., None] * MOBA_BLOCK + r
        mask = ok[..., None] & (kpos <= tpos[None, :, None, None, None])
        logits = jnp.einsum('bqgjd,bqgsrd->bqgjsr', qb, ks).astype(jnp.float32) * HEAD_DIM ** -0.5
        bucket = rel_bucket(tpos[None, :, None, None, None] - kpos)
        bias = bias_g[gidx[..., None], bucket]
        bias = jnp.moveaxis(bias, -1, 3).astype(jnp.float32)
        logits = jnp.where(mask[:, :, :, None], logits + bias, -jnp.inf)
        logits = logits.reshape(B, QBLK_B, N_KV_HEADS, GROUP, S * MOBA_BLOCK)
        p = jax.nn.softmax(logits, axis=-1).astype(vs.dtype)
        p = p.reshape(B, QBLK_B, N_KV_HEADS, GROUP, S, MOBA_BLOCK)
        o = jnp.einsum('bqgjsr,bqgsrd->bqgjd', p, vs)
        return o.reshape(B, QBLK_B, ATTN_WIDTH)

    o = lax.map(block, jnp.arange(nblk))
    o = o.transpose(1, 0, 2, 3).reshape(B, T, ATTN_WIDTH)
    return x + (o * jax.nn.silu(gate)) @ w_out


def setup_inputs(seed: int = 0) -> dict:
    key = jax.random.key(seed)
    ks = jax.random.split(key, 16)
    f32 = jnp.float32

    def nrm(k, shape, scale):
        return jax.random.normal(k, shape, f32) * scale

    def gain(k, shape):
        return 1.0 + 0.02 * jax.random.normal(k, shape, f32)

    return {
        "x": jax.random.normal(ks[0], (BATCH, SEQ, D_MODEL), f32),
        "norm_a_g": gain(ks[1], (N_A_LAYERS, D_MODEL)),
        "w_in_a": nrm(ks[2], (N_A_LAYERS, D_MODEL, A_COLS), D_MODEL ** -0.5),
        "qn_a_g": gain(ks[3], (N_A_LAYERS, HEAD_DIM)),
        "kn_a_g": gain(ks[4], (N_A_LAYERS, HEAD_DIM)),
        "w_out_a": nrm(ks[5], (N_A_LAYERS, ATTN_WIDTH, D_MODEL), ATTN_WIDTH ** -0.5),
        "rel_bias": nrm(ks[6], (REL_BUCKETS, N_HEADS), 0.5),
        "norm_kv_g": gain(ks[7], (D_MODEL,)),
        "w_kv": nrm(ks[8], (D_MODEL, 2 * KV_WIDTH), D_MODEL ** -0.5),
        "kn_b_g": gain(ks[9], (HEAD_DIM,)),
        "norm_b_g": gain(ks[10], (N_B_LAYERS, D_MODEL)),
        "w_in_b": nrm(ks[11], (N_B_LAYERS, D_MODEL, 2 * ATTN_WIDTH), D_MODEL ** -0.5),
        "qn_b_g": gain(ks[12], (N_B_LAYERS, HEAD_DIM)),
        "w_out_b": nrm(ks[13], (N_B_LAYERS, ATTN_WIDTH, D_MODEL), ATTN_WIDTH ** -0.5),
    }


def reference(x, norm_a_g, w_in_a, qn_a_g, kn_a_g, w_out_a, rel_bias, norm_kv_g, w_kv,
              kn_b_g, norm_b_g, w_in_b, qn_b_g, w_out_b):
    h = x
    kb = vb = kmean = None
    for layer in range(DEPTH):
        if layer < N_A_LAYERS:
            h = dsa_layer(h, norm_a_g[layer], w_in_a[layer], qn_a_g[layer], kn_a_g[layer],
                          w_out_a[layer], rel_bias)
            if layer == N_A_LAYERS - 1:
                kb, vb, kmean = shared_kv(h, norm_kv_g, w_kv, kn_b_g)
        else:
            j = layer - N_A_LAYERS
            h = moba_layer(h, norm_b_g[j], w_in_b[j], qn_b_g[j], w_out_b[j], rel_bias,
                           kb, vb, kmean)
    return h
```

```python
import functools
import math

import jax
import jax.numpy as jnp
import numpy as np
from jax import lax
from jax.experimental import pallas as pl
from jax.experimental.pallas import tpu as pltpu

D_MODEL = 1024
N_HEADS = 16
HEAD_DIM = 64
N_KV_HEADS = 4
GROUP = 4
ATTN_WIDTH = 1024
KV_WIDTH = 256
IDX_HEADS = 8
IDX_DIM = 64
DSA_TOPK_MAX = 256
MOBA_BLOCK = 256
MOBA_TOPK = 3
REL_BUCKETS = 32
REL_MAX_DIST = 128
EPS = 1e-6

LANES = 128
HALF = LANES // 2
TQ = 256
TM_A = 512
N_PAIRS = N_HEADS // 2
LOG2E = 1.4426950408889634
Q_SCALE = HEAD_DIM ** -0.5 * LOG2E
NEG_BIG = -1e30
VMEM_LIMIT = 56 * 1024 * 1024

F32 = jnp.float32
BF16 = jnp.bfloat16
I32 = jnp.int32


def _bucket_starts():
    max_exact = REL_BUCKETS // 2
    n = np.arange(0, 4096)
    nf = np.maximum(n, 1).astype(np.float64)
    large = max_exact + (np.log(nf / max_exact) / math.log(REL_MAX_DIST / max_exact)
                         * (REL_BUCKETS - max_exact)).astype(np.int64)
    bucket = np.where(n < max_exact, n, np.minimum(large, REL_BUCKETS - 1))
    nf32 = np.maximum(n, 1).astype(np.float32)
    large32 = max_exact + (np.log(nf32 / np.float32(max_exact)) / np.float32(math.log(REL_MAX_DIST / max_exact))
                           * np.float32(REL_BUCKETS - max_exact)).astype(np.int32)
    bucket32 = np.where(n < max_exact, n, np.minimum(large32, REL_BUCKETS - 1))
    assert (bucket == bucket32).all() and (np.diff(bucket) >= 0).all()
    return [int(np.argmax(bucket >= b)) for b in range(REL_BUCKETS)]


BUCKET_STARTS = _bucket_starts()


def _cparams(n_axes):
    return pltpu.CompilerParams(dimension_semantics=("arbitrary",) * n_axes,
                                vmem_limit_bytes=VMEM_LIMIT)


def _dot_t(a, b):
    return lax.dot_general(a, b, (((1,), (1,)), ((), ())), preferred_element_type=F32)


def _dot(a, b):
    return jnp.dot(a, b, preferred_element_type=F32)


def _lane(shape):
    return lax.broadcasted_iota(I32, shape, 1)


def _two(x):
    return jnp.concatenate([x, x], axis=1)


def _bias_kernel(rb_ref, out_ref):
    kind = pl.program_id(0)
    h = pl.program_id(1)
    row = lax.broadcasted_iota(I32, (TQ, TQ), 0)
    col = lax.broadcasted_iota(I32, (TQ, TQ), 1)
    n = row - col + (1 - kind) * TQ
    last = rb_ref[REL_BUCKETS - 1, h]
    tile = jnp.full((TQ, TQ), last, F32)
    for b in range(REL_BUCKETS - 2, -1, -1):
        tile = jnp.where(n < BUCKET_STARTS[b + 1], rb_ref[b, h], tile)
    tile = (tile - last) * LOG2E
    out_ref[0, 0] = jnp.where(n < 0, -jnp.inf, tile)


def _bias_tiles(rel_bias):
    return pl.pallas_call(
        _bias_kernel,
        grid=(2, N_HEADS),
        in_specs=[pl.BlockSpec(memory_space=pltpu.SMEM)],
        out_specs=pl.BlockSpec((1, 1, TQ, TQ), lambda k, h: (k, h, 0, 0)),
        out_shape=jax.ShapeDtypeStruct((2, N_HEADS, TQ, TQ), F32),
        compiler_params=_cparams(2), name="bias_tiles",
    )(rel_bias)


def _rms_rows(x):
    return lax.rsqrt(jnp.mean(x * x, axis=-1, keepdims=True) + EPS)


def _half_norm(t, gain2):
    lo = _lane(t.shape) < HALF
    s = t * t
    s_lo = jnp.sum(jnp.where(lo, s, 0.0), axis=1, keepdims=True)
    s_hi = jnp.sum(jnp.where(lo, 0.0, s), axis=1, keepdims=True)
    r_lo = lax.rsqrt(s_lo * (1.0 / HEAD_DIM) + EPS)
    r_hi = lax.rsqrt(s_hi * (1.0 / HEAD_DIM) + EPS)
    return t * jnp.where(lo, r_lo, r_hi) * gain2


def _place(t, src_half, dst_half):
    if src_half != dst_half:
        t = pltpu.roll(t, HALF, 1)
    keep = (_lane(t.shape) < HALF) if dst_half == 0 else (_lane(t.shape) >= HALF)
    return jnp.where(keep, t, 0.0)


def _store_q16(q_ref, yq, gain2):
    for p in range(N_PAIRS):
        t = _half_norm(yq[:, p * LANES:(p + 1) * LANES], gain2) * Q_SCALE
        dst = (p // 2) % 2
        q_ref[0, 2 * p] = _place(t, 0, dst).astype(BF16)
        q_ref[0, 2 * p + 1] = _place(t, 1, dst).astype(BF16)


def _store_vp(vp_ref, yv):
    for m in range(2):
        t = yv[:, m * LANES:(m + 1) * LANES]
        for src in range(2):
            g = 2 * m + src
            for e in range(2):
                vp_ref[0, 2 * g + e] = _place(t, src, e).astype(BF16)


def _silu(y):
    return y * jax.nn.sigmoid(y)


A_Q, A_K, A_V, A_G, A_IQ, A_T = 0, 1024, 1280, 1536, 2560, 3072
A_PACKED = 3328


def _proj_a_kernel(x_ref, g_ref, w_ref, qg_ref, kg_ref,
                   q_ref, k_ref, vp_ref, sg_ref, iq_ref, iw_ref, ikp_ref):
    x = x_ref[0]
    h = (x * _rms_rows(x) * g_ref[...]).astype(BF16)
    _store_q16(q_ref, _dot(h, w_ref[:, A_Q:A_K]), qg_ref[...])
    yk = _dot(h, w_ref[:, A_K:A_V])
    for m in range(2):
        k_ref[0, m] = _half_norm(yk[:, m * LANES:(m + 1) * LANES], kg_ref[...]).astype(BF16)
    _store_vp(vp_ref, _dot(h, w_ref[:, A_V:A_G]))
    sg_ref[0] = _silu(_dot(h, w_ref[:, A_G:A_IQ])).astype(BF16)
    yiq = _dot(h, w_ref[:, A_IQ:A_T])
    for m in range(4):
        iq_ref[0, m] = yiq[:, m * LANES:(m + 1) * LANES].astype(BF16)
    yt = _dot(h, w_ref[:, A_T:A_PACKED])
    iw_ref[0] = yt[:, :LANES] * (IDX_HEADS ** -0.5 * IDX_DIM ** -0.5)
    ik = yt[:, LANES:]
    ikp_ref[0, 0] = ik.astype(BF16)
    ikp_ref[0, 1] = pltpu.roll(ik, HALF, 1).astype(BF16)


def _const_spec(shape):
    nd = len(shape)
    return pl.BlockSpec(shape, lambda *_: (0,) * nd, pipeline_mode=pl.Buffered(1))


def _proj_a(x, norm_g, w_in, qn_g, kn_g):
    B, T, _ = x.shape
    pad = lambda w: jnp.pad(w, ((0, 0), (0, LANES - w.shape[1])))
    w = jnp.concatenate([w_in[:, :3072], pad(w_in[:, 3072:3080]), pad(w_in[:, 3080:3144])],
                        axis=1).astype(BF16)
    g2 = lambda g: jnp.concatenate([g, g]).reshape(1, LANES)
    row = lambda n: pl.BlockSpec((1, TM_A, n), lambda b, i: (b, i, 0))
    tiles = lambda n: pl.BlockSpec((1, n, TM_A, LANES), lambda b, i: (b, 0, i, 0))
    return pl.pallas_call(
        _proj_a_kernel,
        grid=(B, T // TM_A),
        in_specs=[row(D_MODEL), _const_spec((1, D_MODEL)), _const_spec((D_MODEL, A_PACKED)),
                  _const_spec((1, LANES)), _const_spec((1, LANES))],
        out_specs=[tiles(N_HEADS), tiles(2), tiles(2 * N_KV_HEADS), row(ATTN_WIDTH),
                   tiles(4), row(LANES), tiles(2)],
        out_shape=[jax.ShapeDtypeStruct((B, N_HEADS, T, LANES), BF16),
                   jax.ShapeDtypeStruct((B, 2, T, LANES), BF16),
                   jax.ShapeDtypeStruct((B, 2 * N_KV_HEADS, T, LANES), BF16),
                   jax.ShapeDtypeStruct((B, T, ATTN_WIDTH), BF16),
                   jax.ShapeDtypeStruct((B, 4, T, LANES), BF16),
                   jax.ShapeDtypeStruct((B, T, LANES), F32),
                   jax.ShapeDtypeStruct((B, 2, T, LANES), BF16)],
        compiler_params=_cparams(2), name="proj_a",
    )(x, norm_g.reshape(1, D_MODEL), w, g2(qn_g), g2(kn_g))


def _proj_b_kernel(x_ref, gkv_ref, gb_ref, wkv_ref, wb_ref, kg_ref, qg_ref,
                   q_ref, k_ref, vp_ref, sg_ref, km_ref):
    x = x_ref[0]
    xr = x * _rms_rows(x)
    hkv = (xr * gkv_ref[...]).astype(BF16)
    hb = (xr * gb_ref[...]).astype(BF16)
    ykv = _dot(hkv, wkv_ref[...])
    means = []
    for m in range(2):
        kn = _half_norm(ykv[:, m * LANES:(m + 1) * LANES], kg_ref[...])
        k_ref[0, m] = kn.astype(BF16)
        means.append(jnp.sum(kn, axis=0, keepdims=True) * (1.0 / MOBA_BLOCK))
    km_ref[0, 0] = jnp.concatenate(means, axis=1)
    _store_vp(vp_ref, ykv[:, KV_WIDTH:])
    _store_q16(q_ref, _dot(hb, wb_ref[:, :ATTN_WIDTH]), qg_ref[...])
    sg_ref[0] = _silu(_dot(hb, wb_ref[:, ATTN_WIDTH:])).astype(BF16)


def _proj_b(x, norm_kv_g, w_kv, kn_g, norm_b_g, w_in_b, qn_g):
    B, T, _ = x.shape
    assert T % MOBA_BLOCK == 0 and TQ == MOBA_BLOCK
    g2 = lambda g: jnp.concatenate([g, g]).reshape(1, LANES)
    row = lambda n: pl.BlockSpec((1, TQ, n), lambda b, i: (b, i, 0))
    tiles = lambda n: pl.BlockSpec((1, n, TQ, LANES), lambda b, i: (b, 0, i, 0))
    return pl.pallas_call(
        _proj_b_kernel,
        grid=(B, T // TQ),
        in_specs=[row(D_MODEL), _const_spec((1, D_MODEL)), _const_spec((1, D_MODEL)),
                  _const_spec((D_MODEL, 2 * KV_WIDTH)), _const_spec((D_MODEL, 2 * ATTN_WIDTH)),
                  _const_spec((1, LANES)), _const_spec((1, LANES))],
        out_specs=[tiles(N_HEADS), tiles(2), tiles(2 * N_KV_HEADS), row(ATTN_WIDTH),
                   pl.BlockSpec((1, 1, 1, KV_WIDTH), lambda b, i: (b, i, 0, 0))],
        out_shape=[jax.ShapeDtypeStruct((B, N_HEADS, T, LANES), BF16),
                   jax.ShapeDtypeStruct((B, 2, T, LANES), BF16),
                   jax.ShapeDtypeStruct((B, 2 * N_KV_HEADS, T, LANES), BF16),
                   jax.ShapeDtypeStruct((B, T, ATTN_WIDTH), BF16),
                   jax.ShapeDtypeStruct((B, T // TQ, 1, KV_WIDTH), F32)],
        compiler_params=_cparams(2), name="proj_b",
    )(x, norm_kv_g.reshape(1, D_MODEL), norm_b_g.reshape(1, D_MODEL),
      w_kv.astype(BF16), w_in_b.astype(BF16), g2(kn_g), g2(qn_g))


def _attend_pairs(i, q_ref, k_ref, vp_ref, bias_ref, madd_fn, m_ref, l_ref, acc_ref, o_ref):
    lo = _lane((TQ, LANES)) < HALF

    def pair_body(p, carry):
        g = p // 2
        m_ref[...] = jnp.full(m_ref.shape, NEG_BIG, F32)
        l_ref[...] = jnp.zeros(l_ref.shape, F32)
        acc_ref[...] = jnp.zeros(acc_ref.shape, F32)

        def chunk(c, bias_kind):
            kt = k_ref[0, g // 2, pl.ds(pl.multiple_of(c * TQ, TQ), TQ), :]
            madd = madd_fn(c, g, bias_kind == 1)
            alphas, pv = [], None
            for e in range(2):
                hd = 2 * p + e
                s = _dot_t(q_ref[0, hd], kt)
                if madd is not None:
                    s = s + madd
                if bias_kind is not None:
                    s = s + bias_ref[bias_kind, hd]
                m_prev = m_ref[e]
                m_new = jnp.maximum(m_prev, jnp.max(s, axis=1, keepdims=True))
                alpha = jnp.exp2(m_prev - m_new)
                pr = jnp.exp2(s - _two(m_new))
                l_ref[e] = alpha * l_ref[e] + pr[:, :LANES] + pr[:, LANES:]
                m_ref[e] = m_new
                alphas.append(alpha)
                v = vp_ref[0, 2 * g + e, pl.ds(pl.multiple_of(c * TQ, TQ), TQ), :]
                d = _dot(pr.astype(BF16), v)
                pv = d if pv is None else pv + d
            acc_ref[...] = acc_ref[...] * jnp.where(lo, alphas[0], alphas[1]) + pv

        def far_body(c, carry):
            chunk(c, None)
            return carry

        lax.fori_loop(0, jnp.maximum(i - 1, 0), far_body, 0)

        @pl.when(i >= 1)
        def _():
            chunk(i - 1, 0)

        chunk(i, 1)
        inv = [1.0 / jnp.sum(l_ref[e], axis=1, keepdims=True) for e in range(2)]
        o_ref[p] = acc_ref[...] * jnp.where(lo, inv[0], inv[1])
        return carry

    lax.fori_loop(0, N_PAIRS, pair_body, 0)


def _gated_out(o_ref, sg_ref, x_ref, wout_ref, out_ref):
    og = [(o_ref[p] * sg_ref[0, :, p * LANES:(p + 1) * LANES].astype(F32)).astype(BF16)
          for p in range(N_PAIRS)]
    out_ref[0] = x_ref[0] + _dot(jnp.concatenate(og, axis=1), wout_ref[...])


ATTN_SCRATCH = [pltpu.VMEM((2, TQ, LANES), F32),
                pltpu.VMEM((2, TQ, LANES), F32),
                pltpu.VMEM((TQ, LANES), F32),
                pltpu.VMEM((N_PAIRS, TQ, LANES), F32)]


INT_MIN = -2 ** 31
KEY_NEG_INF = -0x7F800000


def _key_to_float(k):
    bits = jnp.where(k < 0, jnp.int32(INT_MIN) - k, k)
    return lax.bitcast_convert_type(bits, F32)


def _dsa_kernel(nt, topk, q_ref, sg_ref, x_ref, iq_ref, iw_ref, k_ref, vp_ref, ikp_ref,
                wout_ref, bias_ref, out_ref,
                sc_ref, madd_ref, wb_ref, m_ref, l_ref, acc_ref, o_ref):
    i = pl.program_id(1)
    row = lax.broadcasted_iota(I32, (TQ, TQ), 0)
    col = lax.broadcasted_iota(I32, (TQ, TQ), 1)

    iw = iw_ref[0]
    for h in range(IDX_HEADS):
        wb_ref[h] = jnp.broadcast_to(iw[:, h:h + 1], (TQ, LANES))

    def score_chunk(c):
        acc = jnp.zeros((TQ, TQ), F32)
        for h in range(IDX_HEADS):
            ik = ikp_ref[0, h % 2, pl.ds(pl.multiple_of(c * TQ, TQ), TQ), :]
            d = _dot_t(iq_ref[0, h // 2], ik)
            acc = acc + _two(wb_ref[h]) * jnp.maximum(d, 0.0)
        return acc

    def score_body(c, carry):
        sc_ref[c] = score_chunk(c)
        return carry

    lax.fori_loop(0, i, score_body, 0)
    sc_ref[i] = jnp.where(col > row, -jnp.inf, score_chunk(i))

    lane = _lane((TQ, LANES))

    def count(pred):
        def body(c, tot):
            for hf in range(2):
                s = sc_ref[c, :, hf * LANES:(hf + 1) * LANES]
                tot = tot + jnp.where(pred(s, c * TQ + hf * LANES + lane), 1.0, 0.0)
            return tot
        tot = lax.fori_loop(0, i + 1, body, jnp.zeros((TQ, LANES), F32))
        return jnp.sum(tot, axis=1, keepdims=True)

    def bit_body(b, carry):
        u, cnt_t = carry
        uc = u | lax.shift_left(jnp.int32(1), 31 - b)
        in_range = (uc >= 0) | (uc <= -(1 << 24))
        cand = _key_to_float(jnp.int32(KEY_NEG_INF) + uc)
        cnt = count(lambda s, pos: s >= cand)
        ok = in_range & (cnt >= topk)
        return jnp.where(ok, uc, u), jnp.where(ok, cnt, cnt_t)

    u0 = jnp.zeros((TQ, 1), I32)
    cnt0 = jnp.full((TQ, 1), float(nt * TQ), F32)
    u, cnt_t = lax.fori_loop(0, 32, bit_body, (u0, cnt0))
    thr = _key_to_float(jnp.int32(KEY_NEG_INF) + u)

    excess = (cnt_t > topk) & (thr > -jnp.inf)
    any_excess = jnp.max(jnp.where(excess, 1.0, 0.0)) > 0.0

    @pl.when(jnp.logical_not(any_excess))
    def _():
        def body(c, carry):
            madd_ref[c] = jnp.where(sc_ref[c] >= thr, 0.0, -jnp.inf)
            return carry
        lax.fori_loop(0, i + 1, body, 0)

    @pl.when(any_excess)
    def _():
        need = topk - count(lambda s, pos: s > thr)

        def pos_body(b, qpos):
            cand = qpos | lax.shift_left(jnp.int32(1), 10 - b)
            f = count(lambda s, pos: (s == thr) & (pos < cand))
            return jnp.where(f < need, cand, qpos)

        assert nt * TQ <= 2048
        qpos = lax.fori_loop(0, 11, pos_body, jnp.zeros((TQ, 1), I32))
        qpos = jnp.where(excess, qpos, jnp.int32(2 ** 30))

        def body(c, carry):
            s = sc_ref[c]
            sel = (s > thr) | ((s == thr) & (c * TQ + col <= qpos))
            madd_ref[c] = jnp.where(sel, 0.0, -jnp.inf)
            return carry
        lax.fori_loop(0, i + 1, body, 0)

    _attend_pairs(i, q_ref, k_ref, vp_ref, bias_ref, lambda c, g, diag: madd_ref[c],
                  m_ref, l_ref, acc_ref, o_ref)
    _gated_out(o_ref, sg_ref, x_ref, wout_ref, out_ref)


def _attn_specs(T):
    tile = lambda n: pl.BlockSpec((1, TQ, n), lambda b, i: (b, i, 0))
    tiles = lambda n: pl.BlockSpec((1, n, TQ, LANES), lambda b, i: (b, 0, i, 0))
    whole = lambda n: pl.BlockSpec((1, n, T, LANES), lambda b, i: (b, 0, 0, 0))
    return tile, tiles, whole


def _dsa_attention(x, q16, k2, vp, sg, iq, iw, ikp, w_out, bias):
    B, T, _ = x.shape
    nt = T // TQ
    topk = min(DSA_TOPK_MAX, T // 4)
    tile, tiles, whole = _attn_specs(T)
    return pl.pallas_call(
        functools.partial(_dsa_kernel, nt, topk),
        grid=(B, nt),
        in_specs=[tiles(N_HEADS), tile(ATTN_WIDTH), tile(D_MODEL), tiles(4), tile(LANES),
                  whole(2), whole(2 * N_KV_HEADS), whole(2),
                  _const_spec((ATTN_WIDTH, D_MODEL)), _const_spec((2, N_HEADS, TQ, TQ))],
        out_specs=tile(D_MODEL),
        out_shape=jax.ShapeDtypeStruct((B, T, D_MODEL), F32),
        scratch_shapes=[pltpu.VMEM((nt, TQ, TQ), F32), pltpu.VMEM((nt, TQ, TQ), F32),
                        pltpu.VMEM((IDX_HEADS, TQ, LANES), F32)] + ATTN_SCRATCH,
        compiler_params=_cparams(2), name="dsa_attention",
    )(q16, sg, x, iq, iw, k2, vp, ikp, w_out.astype(BF16), bias)


def _split_bf16(a):
    hi = a.astype(BF16)
    return hi, (a - hi.astype(F32)).astype(BF16)


def _moba_kernel(nt, nsel, q_ref, sg_ref, x_ref, k_ref, vp_ref, km_ref, wout_ref, bias_ref,
                 out_ref, bits_ref, m_ref, l_ref, acc_ref, o_ref):
    i = pl.program_id(1)
    lane = _lane((TQ, LANES))

    km = jnp.concatenate([km_ref[0], jnp.zeros((LANES - nt, KV_WIDTH), F32)], axis=0)
    for g in range(N_KV_HEADS):
        qs = q_ref[0, GROUP * g].astype(F32)
        for j in range(1, GROUP):
            qs = qs + q_ref[0, GROUP * g + j].astype(F32)
        q_hi, q_lo = _split_bf16(qs)
        m_hi, m_lo = _split_bf16(km[:, (g // 2) * LANES:(g // 2 + 1) * LANES])
        gs = _dot_t(q_hi, m_hi) + (_dot_t(q_lo, m_hi) + _dot_t(q_hi, m_lo))
        rank = jnp.zeros((TQ, LANES), F32)
        for n in range(nt):
            gn = gs[:, n:n + 1]
            beats = (gn > gs) | ((gn == gs) & (n < lane))
            rank = rank + jnp.where(beats, jnp.where(n < i, 1.0, 0.0), 0.0)
        sel = (rank < nsel) & (lane < i)
        weight = jnp.left_shift(jnp.int32(1), jnp.minimum(lane, nt)).astype(F32)
        bits = jnp.sum(jnp.where(sel, weight, 0.0), axis=1, keepdims=True)
        bits_ref[g] = jnp.broadcast_to(bits.astype(I32), (TQ, LANES))

    def madd_fn(c, g, diag):
        if diag:
            return None
        on = (jnp.right_shift(bits_ref[g], c) & 1) == 1
        return _two(jnp.where(on, 0.0, -jnp.inf))

    _attend_pairs(i, q_ref, k_ref, vp_ref, bias_ref, madd_fn, m_ref, l_ref, acc_ref, o_ref)
    _gated_out(o_ref, sg_ref, x_ref, wout_ref, out_ref)


def _moba_attention(x, q16, k2, vp, sg, kmean, w_out, bias):
    B, T, _ = x.shape
    nt = T // TQ
    nsel = min(MOBA_TOPK, nt - 1)
    tile, tiles, whole = _attn_specs(T)
    return pl.pallas_call(
        functools.partial(_moba_kernel, nt, nsel),
        grid=(B, nt),
        in_specs=[tiles(N_HEADS), tile(ATTN_WIDTH), tile(D_MODEL),
                  whole(2), whole(2 * N_KV_HEADS),
                  pl.BlockSpec((1, nt, KV_WIDTH), lambda b, i: (b, 0, 0)),
                  _const_spec((ATTN_WIDTH, D_MODEL)), _const_spec((2, N_HEADS, TQ, TQ))],
        out_specs=tile(D_MODEL),
        out_shape=jax.ShapeDtypeStruct((B, T, D_MODEL), F32),
        scratch_shapes=[pltpu.VMEM((N_KV_HEADS, TQ, LANES), I32)] + ATTN_SCRATCH,
        compiler_params=_cparams(2), name="moba_attention",
    )(q16, sg, x, k2, vp, kmean, w_out.astype(BF16), bias)


def kernel(x, norm_a_g, w_in_a, qn_a_g, kn_a_g, w_out_a, rel_bias, norm_kv_g, w_kv, kn_b_g,
           norm_b_g, w_in_b, qn_b_g, w_out_b):
    B, T, _ = x.shape
    assert norm_a_g.shape[0] == 1 and norm_b_g.shape[0] == 1
    bias = _bias_tiles(rel_bias)
    q16, k2, vp, sg, iq, iw, ikp = _proj_a(x, norm_a_g[0], w_in_a[0], qn_a_g[0], kn_a_g[0])
    h = _dsa_attention(x, q16, k2, vp, sg, iq, iw, ikp, w_out_a[0], bias)
    q16, k2, vp, sg, kmean = _proj_b(h, norm_kv_g, w_kv, kn_b_g, norm_b_g[0], w_in_b[0], qn_b_g[0])
    return _moba_attention(h, q16, k2, vp, sg, kmean.reshape(B, T // TQ, KV_WIDTH),
                           w_out_b[0], bias)
```

```python
import functools
import math

import jax
import jax.numpy as jnp
import numpy as np
from jax import lax
from jax.experimental import pallas as pl
from jax.experimental.pallas import tpu as pltpu

D_MODEL = 1024
N_HEADS = 16
HEAD_DIM = 64
N_KV_HEADS = 4
GROUP = 4
ATTN_WIDTH = 1024
KV_WIDTH = 256
IDX_HEADS = 8
IDX_DIM = 64
DSA_TOPK_MAX = 256
MOBA_BLOCK = 256
MOBA_TOPK = 3
REL_BUCKETS = 32
REL_MAX_DIST = 128
EPS = 1e-6

LANES = 128
SUBLANES = 8
HALF = LANES // 2
TQ = 256
TM_A = 512
N_PAIRS = N_HEADS // 2
HB = 8
VT_ROWS = LANES + 2 * SUBLANES
LOG2E = 1.4426950408889634
Q_SCALE = HEAD_DIM ** -0.5 * LOG2E
NEG_BIG = -1e30
VMEM_LIMIT = 56 * 1024 * 1024

F32 = jnp.float32
BF16 = jnp.bfloat16
I32 = jnp.int32


def _bucket_starts():
    max_exact = REL_BUCKETS // 2
    n = np.arange(0, 4096)
    nf = np.maximum(n, 1).astype(np.float64)
    large = max_exact + (np.log(nf / max_exact) / math.log(REL_MAX_DIST / max_exact)
                         * (REL_BUCKETS - max_exact)).astype(np.int64)
    bucket = np.where(n < max_exact, n, np.minimum(large, REL_BUCKETS - 1))
    nf32 = np.maximum(n, 1).astype(np.float32)
    large32 = max_exact + (np.log(nf32 / np.float32(max_exact)) / np.float32(math.log(REL_MAX_DIST / max_exact))
                           * np.float32(REL_BUCKETS - max_exact)).astype(np.int32)
    bucket32 = np.where(n < max_exact, n, np.minimum(large32, REL_BUCKETS - 1))
    assert (bucket == bucket32).all() and (np.diff(bucket) >= 0).all()
    return [int(np.argmax(bucket >= b)) for b in range(REL_BUCKETS)]


BUCKET_STARTS = _bucket_starts()


def _cparams(n_axes):
    return pltpu.CompilerParams(dimension_semantics=("arbitrary",) * n_axes,
                                vmem_limit_bytes=VMEM_LIMIT)


def _dot_t(a, b):
    return lax.dot_general(a, b, (((1,), (1,)), ((), ())), preferred_element_type=F32)


def _dot(a, b):
    return jnp.dot(a, b, preferred_element_type=F32)


def _const_spec(shape):
    nd = len(shape)
    return pl.BlockSpec(shape, lambda *_: (0,) * nd, pipeline_mode=pl.Buffered(1))


def _bias_kernel(rb_ref, out_ref):
    kind = pl.program_id(0)
    h = pl.program_id(1)
    key = lax.broadcasted_iota(I32, (TQ, TQ), 0)
    qry = lax.broadcasted_iota(I32, (TQ, TQ), 1)
    n = qry - key + (1 - kind) * TQ
    last = rb_ref[REL_BUCKETS - 1, h]
    tile = jnp.full((TQ, TQ), last, F32)
    for b in range(REL_BUCKETS - 2, -1, -1):
        tile = jnp.where(n < BUCKET_STARTS[b + 1], rb_ref[b, h], tile)
    tile = (tile - last) * LOG2E
    out_ref[0, 0] = jnp.where(n < 0, -jnp.inf, tile)


def _bias_tiles(rel_bias):
    return pl.pallas_call(
        _bias_kernel,
        grid=(2, N_HEADS),
        in_specs=[pl.BlockSpec(memory_space=pltpu.SMEM)],
        out_specs=pl.BlockSpec((1, 1, TQ, TQ), lambda k, h: (k, h, 0, 0)),
        out_shape=jax.ShapeDtypeStruct((2, N_HEADS, TQ, TQ), F32),
        compiler_params=_cparams(2), name="bias_tiles",
    )(rel_bias)


def _rms_rows(x):
    return lax.rsqrt(jnp.mean(x * x, axis=-1, keepdims=True) + EPS)


def _half_norm(t, gain2):
    lo = lax.broadcasted_iota(I32, t.shape, 1) < HALF
    s = t * t
    s_lo = jnp.sum(jnp.where(lo, s, 0.0), axis=1, keepdims=True)
    s_hi = jnp.sum(jnp.where(lo, 0.0, s), axis=1, keepdims=True)
    r_lo = lax.rsqrt(s_lo * (1.0 / HEAD_DIM) + EPS)
    r_hi = lax.rsqrt(s_hi * (1.0 / HEAD_DIM) + EPS)
    return t * jnp.where(lo, r_lo, r_hi) * gain2


def _in_half(slab, half):
    z = jnp.zeros_like(slab)
    return jnp.concatenate([slab, z] if half == 0 else [z, slab], axis=0)


def _store_qT(qT_ref, yqT, gain_col):
    for h in range(N_HEADS):
        slab = yqT[h * HEAD_DIM:(h + 1) * HEAD_DIM, :]
        r = lax.rsqrt(jnp.sum(slab * slab, axis=0, keepdims=True) * (1.0 / HEAD_DIM) + EPS)
        qn = slab * r * gain_col * Q_SCALE
        qT_ref[0, h] = _in_half(qn, (h // GROUP) % 2).astype(BF16)


def _store_vT(vT_ref, yvT):
    m = yvT.shape[1]
    ones = jnp.ones((SUBLANES, m), F32)
    zeros = jnp.zeros((SUBLANES, m), F32)
    for g in range(N_KV_HEADS):
        slab = yvT[g * HEAD_DIM:(g + 1) * HEAD_DIM, :]
        for e in range(2):
            tail = [ones, zeros] if e == 0 else [zeros, ones]
            tile = jnp.concatenate([_in_half(slab, e)] + tail, axis=0).astype(BF16)
            for cc in range(m // TQ):
                vT_ref[0, 2 * g + e, cc] = tile[:, cc * TQ:(cc + 1) * TQ]


def _silu(y):
    return y * jax.nn.sigmoid(y)


def _proj_out_specs(tm, T):
    nt = T // TQ
    specs = dict(
        qT=(pl.BlockSpec((1, N_HEADS, LANES, tm), lambda b, i: (b, 0, 0, i)),
            lambda B: jax.ShapeDtypeStruct((B, N_HEADS, LANES, T), BF16)),
        k2=(pl.BlockSpec((1, 2, tm, LANES), lambda b, i: (b, 0, i, 0)),
            lambda B: jax.ShapeDtypeStruct((B, 2, T, LANES), BF16)),
        vT=(pl.BlockSpec((1, 2 * N_KV_HEADS, tm // TQ, VT_ROWS, TQ), lambda b, i: (b, 0, i, 0, 0)),
            lambda B: jax.ShapeDtypeStruct((B, 2 * N_KV_HEADS, nt, VT_ROWS, TQ), BF16)),
        sg=(pl.BlockSpec((1, tm, ATTN_WIDTH), lambda b, i: (b, i, 0)),
            lambda B: jax.ShapeDtypeStruct((B, T, ATTN_WIDTH), BF16)),
    )
    return specs


def _proj_a_kernel(x_ref, g_ref, wq_ref, wn_ref, wr_ref, qg_ref, kg_ref,
                   qT_ref, k_ref, vT_ref, sg_ref, iqT_ref, iwT_ref, ik_ref):
    x = x_ref[0]
    h = (x * _rms_rows(x) * g_ref[...]).astype(BF16)
    _store_qT(qT_ref, _dot_t(wq_ref[...], h), qg_ref[...])
    yn = _dot(h, wn_ref[...])
    for m in range(2):
        k_ref[0, m] = _half_norm(yn[:, m * LANES:(m + 1) * LANES], kg_ref[...]).astype(BF16)
    sg_ref[0] = _silu(yn[:, KV_WIDTH:KV_WIDTH + ATTN_WIDTH]).astype(BF16)
    ik_ref[0] = yn[:, KV_WIDTH + ATTN_WIDTH:].astype(BF16)
    yr = _dot_t(wr_ref[...], h)
    _store_vT(vT_ref, yr[:KV_WIDTH])
    for hh in range(IDX_HEADS):
        slab = yr[KV_WIDTH + hh * IDX_DIM:KV_WIDTH + (hh + 1) * IDX_DIM, :]
        iqT_ref[0, hh] = _in_half(slab, 0).astype(BF16)
    iw0 = KV_WIDTH + IDX_HEADS * IDX_DIM
    iwT_ref[0] = yr[iw0:iw0 + IDX_HEADS, :] * (IDX_HEADS ** -0.5 * IDX_DIM ** -0.5)


def _proj_a(x, norm_g, w_in, qn_g, kn_g):
    B, T, _ = x.shape
    c = np.cumsum([0, ATTN_WIDTH, KV_WIDTH, KV_WIDTH, ATTN_WIDTH, IDX_HEADS * IDX_DIM, IDX_HEADS, IDX_DIM])
    part = lambda j: w_in[:, c[j]:c[j + 1]]
    wq_t = part(0).T.astype(BF16)
    wn = jnp.concatenate([part(1), part(3), jnp.pad(part(6), ((0, 0), (0, LANES - IDX_DIM)))],
                         axis=1).astype(BF16)
    wr_t = jnp.concatenate([part(2), part(4), jnp.pad(part(5), ((0, 0), (0, IDX_HEADS)))],
                           axis=1).T.astype(BF16)
    so = _proj_out_specs(TM_A, T)
    outs = [so["qT"], so["k2"], so["vT"], so["sg"],
            (pl.BlockSpec((1, IDX_HEADS, LANES, TM_A), lambda b, i: (b, 0, 0, i)),
             lambda B: jax.ShapeDtypeStruct((B, IDX_HEADS, LANES, T), BF16)),
            (pl.BlockSpec((1, IDX_HEADS, TM_A), lambda b, i: (b, 0, i)),
             lambda B: jax.ShapeDtypeStruct((B, IDX_HEADS, T), F32)),
            (pl.BlockSpec((1, TM_A, LANES), lambda b, i: (b, i, 0)),
             lambda B: jax.ShapeDtypeStruct((B, T, LANES), BF16))]
    return pl.pallas_call(
        _proj_a_kernel,
        grid=(B, T // TM_A),
        in_specs=[pl.BlockSpec((1, TM_A, D_MODEL), lambda b, i: (b, i, 0)),
                  _const_spec((1, D_MODEL)), _const_spec(wq_t.shape), _const_spec(wn.shape),
                  _const_spec(wr_t.shape), _const_spec((HEAD_DIM, 1)), _const_spec((1, LANES))],
        out_specs=[o[0] for o in outs],
        out_shape=[o[1](B) for o in outs],
        compiler_params=_cparams(2), name="proj_a",
    )(x, norm_g.reshape(1, D_MODEL), wq_t, wn, wr_t, qn_g.reshape(HEAD_DIM, 1),
      jnp.concatenate([kn_g, kn_g]).reshape(1, LANES))


def _proj_b_kernel(x_ref, gkv_ref, gb_ref, wk_ref, wv_ref, wq_ref, wg_ref, kg_ref, qg_ref,
                   qT_ref, k_ref, vT_ref, sg_ref, km_ref):
    x = x_ref[0]
    xr = x * _rms_rows(x)
    hkv = (xr * gkv_ref[...]).astype(BF16)
    hb = (xr * gb_ref[...]).astype(BF16)
    yk = _dot(hkv, wk_ref[...])
    means = []
    for m in range(2):
        kn = _half_norm(yk[:, m * LANES:(m + 1) * LANES], kg_ref[...])
        k_ref[0, m] = kn.astype(BF16)
        means.append(jnp.sum(kn, axis=0, keepdims=True) * (1.0 / MOBA_BLOCK))
    km_ref[0, 0] = jnp.concatenate(means, axis=1)
    _store_vT(vT_ref, _dot_t(wv_ref[...], hkv))
    _store_qT(qT_ref, _dot_t(wq_ref[...], hb), qg_ref[...])
    sg_ref[0] = _silu(_dot(hb, wg_ref[...])).astype(BF16)


def _proj_b(x, norm_kv_g, w_kv, kn_g, norm_b_g, w_in_b, qn_g):
    B, T, _ = x.shape
    assert T % MOBA_BLOCK == 0 and TQ == MOBA_BLOCK
    wk = w_kv[:, :KV_WIDTH].astype(BF16)
    wv_t = w_kv[:, KV_WIDTH:].T.astype(BF16)
    wq_t = w_in_b[:, :ATTN_WIDTH].T.astype(BF16)
    wg = w_in_b[:, ATTN_WIDTH:].astype(BF16)
    so = _proj_out_specs(TQ, T)
    outs = [so["qT"], so["k2"], so["vT"], so["sg"],
            (pl.BlockSpec((1, 1, 1, KV_WIDTH), lambda b, i: (b, i, 0, 0)),
             lambda B: jax.ShapeDtypeStruct((B, T // TQ, 1, KV_WIDTH), F32))]
    return pl.pallas_call(
        _proj_b_kernel,
        grid=(B, T // TQ),
        in_specs=[pl.BlockSpec((1, TQ, D_MODEL), lambda b, i: (b, i, 0)),
                  _const_spec((1, D_MODEL)), _const_spec((1, D_MODEL)),
                  _const_spec(wk.shape), _const_spec(wv_t.shape), _const_spec(wq_t.shape),
                  _const_spec(wg.shape), _const_spec((1, LANES)), _const_spec((HEAD_DIM, 1))],
        out_specs=[o[0] for o in outs],
        out_shape=[o[1](B) for o in outs],
        compiler_params=_cparams(2), name="proj_b",
    )(x, norm_kv_g.reshape(1, D_MODEL), norm_b_g.reshape(1, D_MODEL), wk, wv_t, wq_t, wg,
      jnp.concatenate([kn_g, kn_g]).reshape(1, LANES), qn_g.reshape(HEAD_DIM, 1))


def _attend(i, qT_ref, k_ref, vT_ref, bias_ref, mask_fn, m_ref, acc_ref, o_ref):
    groups_per_body = HB // GROUP

    def outer(o, carry):
        m_ref[...] = jnp.full(m_ref.shape, NEG_BIG, F32)
        acc_ref[...] = jnp.zeros(acc_ref.shape, F32)

        def chunk(c, bias_kind):
            koff = pl.multiple_of(c * TQ, TQ)
            for gi in range(groups_per_body):
                g = o * groups_per_body + gi
                kt = k_ref[0, g // 2, pl.ds(koff, TQ), :]
                mask = mask_fn(c, g, bias_kind == 1)
                for pj in range(2):
                    lp = 2 * gi + pj
                    pv, alphas = None, []
                    for e in range(2):
                        hd = 2 * (2 * g + pj) + e
                        s = _dot(kt, qT_ref[0, hd])
                        if mask is not None and mask[0] == "add":
                            s = s + mask[1]
                        if bias_kind is not None:
                            s = s + bias_ref[bias_kind, hd]
                        m_prev = m_ref[2 * lp + e]
                        m_cur = jnp.max(s, axis=0, keepdims=True)
                        if mask is not None and mask[0] == "col":
                            m_cur = jnp.where(mask[1], m_cur, NEG_BIG)
                        m_new = jnp.maximum(m_prev, m_cur)
                        m_sub = m_new
                        if mask is not None and mask[0] == "col":
                            m_sub = jnp.where(mask[1], m_new, jnp.inf)
                        alphas.append(jnp.exp2(m_prev - m_new))
                        m_ref[2 * lp + e] = m_new
                        pr = jnp.exp2(s - m_sub).astype(BF16)
                        d = _dot(vT_ref[0, 2 * g + e, c], pr)
                        pv = d if pv is None else pv + d
                    for r0, r1, e in ((0, HALF, 0), (HALF, LANES, 1),
                                      (LANES, LANES + SUBLANES, 0), (LANES + SUBLANES, VT_ROWS, 1)):
                        acc_ref[lp, r0:r1, :] = acc_ref[lp, r0:r1, :] * alphas[e] + pv[r0:r1, :]

        def far_body(c, carry):
            chunk(c, None)
            return carry

        lax.fori_loop(0, jnp.maximum(i - 1, 0), far_body, 0)

        @pl.when(i >= 1)
        def _():
            chunk(i - 1, 0)

        chunk(i, 1)
        for lp in range(HB // 2):
            inv0 = 1.0 / acc_ref[lp, LANES:LANES + 1, :]
            inv1 = 1.0 / acc_ref[lp, LANES + SUBLANES:LANES + SUBLANES + 1, :]
            o_ref[o * (HB // 2) + lp] = jnp.concatenate(
                [acc_ref[lp, 0:HALF, :] * inv0, acc_ref[lp, HALF:LANES, :] * inv1], axis=0)
        return carry

    lax.fori_loop(0, N_HEADS // HB, outer, 0)


def _gated_out(o_ref, sg_ref, x_ref, wout_ref, out_ref):
    og = [(o_ref[p].T * sg_ref[0, :, p * LANES:(p + 1) * LANES].astype(F32)).astype(BF16)
          for p in range(N_PAIRS)]
    out_ref[0] = x_ref[0] + _dot(jnp.concatenate(og, axis=1), wout_ref[...])


ATTN_SCRATCH = [pltpu.VMEM((HB, 1, TQ), F32),
                pltpu.VMEM((HB // 2, VT_ROWS, TQ), F32),
                pltpu.VMEM((N_PAIRS, LANES, TQ), F32)]


def _attn_specs(T):
    nt = T // TQ
    return dict(
        qT=pl.BlockSpec((1, N_HEADS, LANES, TQ), lambda b, i: (b, 0, 0, i)),
        k2=pl.BlockSpec((1, 2, T, LANES), lambda b, i: (b, 0, 0, 0)),
        vT=pl.BlockSpec((1, 2 * N_KV_HEADS, nt, VT_ROWS, TQ), lambda b, i: (b, 0, 0, 0, 0)),
        row=lambda n: pl.BlockSpec((1, TQ, n), lambda b, i: (b, i, 0)),
        wout=_const_spec((ATTN_WIDTH, D_MODEL)),
        bias=_const_spec((2, N_HEADS, TQ, TQ)),
    )


INT_MIN = -2 ** 31
KEY_NEG_INF = -0x7F800000


def _key_to_float(k):
    bits = jnp.where(k < 0, jnp.int32(INT_MIN) - k, k)
    return lax.bitcast_convert_type(bits, F32)


def _dsa_kernel(nt, topk, qT_ref, sg_ref, x_ref, iqT_ref, iwT_ref, k_ref, vT_ref, ik_ref,
                wout_ref, bias_ref, out_ref, sc_ref, madd_ref, m_ref, acc_ref, o_ref):
    i = pl.program_id(1)
    key = lax.broadcasted_iota(I32, (TQ, TQ), 0)
    qry = lax.broadcasted_iota(I32, (TQ, TQ), 1)
    sub = lax.broadcasted_iota(I32, (SUBLANES, TQ), 0)

    iw = iwT_ref[0]

    def score_chunk(c):
        ik = ik_ref[0, pl.ds(pl.multiple_of(c * TQ, TQ), TQ), :]
        acc = jnp.zeros((TQ, TQ), F32)
        for h in range(IDX_HEADS):
            acc = acc + iw[h:h + 1, :] * jnp.maximum(_dot(ik, iqT_ref[0, h]), 0.0)
        return acc

    def score_body(c, carry):
        sc_ref[c] = score_chunk(c)
        return carry

    lax.fori_loop(0, i, score_body, 0)
    sc_ref[i] = jnp.where(key > qry, -jnp.inf, score_chunk(i))

    def count(pred):
        def body(c, tots):
            tots = list(tots)
            for r in range(TQ // SUBLANES):
                s = sc_ref[c, r * SUBLANES:(r + 1) * SUBLANES, :]
                hit = pred(s, c * TQ + r * SUBLANES + sub)
                tots[r % len(tots)] = tots[r % len(tots)] + jnp.where(hit, 1.0, 0.0)
            return tuple(tots)
        zero = jnp.zeros((SUBLANES, TQ), F32)
        t = lax.fori_loop(0, i + 1, body, (zero,) * 4)
        return jnp.sum((t[0] + t[1]) + (t[2] + t[3]), axis=0, keepdims=True)

    def bit_body(b, carry):
        u, cnt_t = carry
        uc = u | lax.shift_left(jnp.int32(1), 31 - b)
        in_range = (uc >= 0) | (uc <= -(1 << 24))
        cand = jnp.broadcast_to(_key_to_float(jnp.int32(KEY_NEG_INF) + uc), (SUBLANES, TQ))
        cnt = count(lambda s, pos: s >= cand)
        ok = in_range & (cnt >= topk)
        return jnp.where(ok, uc, u), jnp.where(ok, cnt, cnt_t)

    u0 = jnp.zeros((1, TQ), I32)
    cnt0 = jnp.full((1, TQ), float(nt * TQ), F32)
    u, cnt_t = lax.fori_loop(0, 32, bit_body, (u0, cnt0))
    thr = _key_to_float(jnp.int32(KEY_NEG_INF) + u)

    excess = (cnt_t > topk) & (thr > -jnp.inf)
    any_excess = jnp.max(jnp.where(excess, 1.0, 0.0)) > 0.0

    @pl.when(jnp.logical_not(any_excess))
    def _():
        def body(c, carry):
            madd_ref[c] = jnp.where(sc_ref[c] >= thr, 0.0, -jnp.inf)
            return carry
        lax.fori_loop(0, i + 1, body, 0)

    @pl.when(any_excess)
    def _():
        thr8 = jnp.broadcast_to(thr, (SUBLANES, TQ))
        need = topk - count(lambda s, pos: s > thr8)

        def pos_body(b, qpos):
            cand = qpos | lax.shift_left(jnp.int32(1), 10 - b)
            f = count(lambda s, pos: (s == thr8) & (pos < cand))
            return jnp.where(f < need, cand, qpos)

        assert nt * TQ <= 2048
        qpos = lax.fori_loop(0, 11, pos_body, jnp.zeros((1, TQ), I32))
        qpos = jnp.where(excess, qpos, jnp.int32(2 ** 30))

        def body(c, carry):
            s = sc_ref[c]
            sel = (s > thr) | ((s == thr) & (c * TQ + key <= qpos))
            madd_ref[c] = jnp.where(sel, 0.0, -jnp.inf)
            return carry
        lax.fori_loop(0, i + 1, body, 0)

    _attend(i, qT_ref, k_ref, vT_ref, bias_ref, lambda c, g, diag: ("add", madd_ref[c]),
            m_ref, acc_ref, o_ref)
    _gated_out(o_ref, sg_ref, x_ref, wout_ref, out_ref)


def _dsa_attention(x, qT, k2, vT, sg, iqT, iwT, ik, w_out, bias):
    B, T, _ = x.shape
    nt = T // TQ
    topk = min(DSA_TOPK_MAX, T // 4)
    sp = _attn_specs(T)
    return pl.pallas_call(
        functools.partial(_dsa_kernel, nt, topk),
        grid=(B, nt),
        in_specs=[sp["qT"], sp["row"](ATTN_WIDTH), sp["row"](D_MODEL),
                  pl.BlockSpec((1, IDX_HEADS, LANES, TQ), lambda b, i: (b, 0, 0, i)),
                  pl.BlockSpec((1, IDX_HEADS, TQ), lambda b, i: (b, 0, i)),
                  sp["k2"], sp["vT"],
                  pl.BlockSpec((1, T, LANES), lambda b, i: (b, 0, 0)),
                  sp["wout"], sp["bias"]],
        out_specs=sp["row"](D_MODEL),
        out_shape=jax.ShapeDtypeStruct((B, T, D_MODEL), F32),
        scratch_shapes=[pltpu.VMEM((nt, TQ, TQ), F32), pltpu.VMEM((nt, TQ, TQ), F32)] + ATTN_SCRATCH,
        compiler_params=_cparams(2), name="dsa_attention",
    )(qT, sg, x, iqT, iwT, k2, vT, ik, w_out.astype(BF16), bias)


def _split_bf16(a):
    hi = a.astype(BF16)
    return hi, (a - hi.astype(F32)).astype(BF16)


def _moba_kernel(nt, nsel, qT_ref, sg_ref, x_ref, k_ref, vT_ref, km_ref, wout_ref, bias_ref,
                 out_ref, sel_ref, m_ref, acc_ref, o_ref):
    i = pl.program_id(1)
    assert nt <= SUBLANES
    blk = lax.broadcasted_iota(I32, (SUBLANES, TQ), 0)

    km = jnp.concatenate([km_ref[0], jnp.zeros((2 * SUBLANES - nt, KV_WIDTH), F32)], axis=0)
    for g in range(N_KV_HEADS):
        qs = qT_ref[0, GROUP * g].astype(F32)
        for j in range(1, GROUP):
            qs = qs + qT_ref[0, GROUP * g + j].astype(F32)
        q_hi, q_lo = _split_bf16(qs)
        m_hi, m_lo = _split_bf16(km[:, (g // 2) * LANES:(g // 2 + 1) * LANES])
        gs = (_dot(m_hi, q_hi) + (_dot(m_hi, q_lo) + _dot(m_lo, q_hi)))[:SUBLANES]
        rank = jnp.zeros((SUBLANES, TQ), F32)
        for n in range(nt):
            gn = gs[n:n + 1, :]
            beats = (gn > gs) | ((gn == gs) & (n < blk))
            rank = rank + jnp.where(beats, jnp.where(n < i, 1.0, 0.0), 0.0)
        sel_ref[g] = jnp.where((rank < nsel) & (blk < i), 1.0, 0.0)

    def mask_fn(c, g, diag):
        if diag:
            return None
        return ("col", sel_ref[g, pl.ds(c, 1), :] > 0.5)

    _attend(i, qT_ref, k_ref, vT_ref, bias_ref, mask_fn, m_ref, acc_ref, o_ref)
    _gated_out(o_ref, sg_ref, x_ref, wout_ref, out_ref)


def _moba_attention(x, qT, k2, vT, sg, kmean, w_out, bias):
    B, T, _ = x.shape
    nt = T // TQ
    nsel = min(MOBA_TOPK, nt - 1)
    sp = _attn_specs(T)
    return pl.pallas_call(
        functools.partial(_moba_kernel, nt, nsel),
        grid=(B, nt),
        in_specs=[sp["qT"], sp["row"](ATTN_WIDTH), sp["row"](D_MODEL), sp["k2"], sp["vT"],
                  pl.BlockSpec((1, nt, KV_WIDTH), lambda b, i: (b, 0, 0)),
                  sp["wout"], sp["bias"]],
        out_specs=sp["row"](D_MODEL),
        out_shape=jax.ShapeDtypeStruct((B, T, D_MODEL), F32),
        scratch_shapes=[pltpu.VMEM((N_KV_HEADS, SUBLANES, TQ), F32)] + ATTN_SCRATCH,
        compiler_params=_cparams(2), name="moba_attention",
    )(qT, sg, x, k2, vT, kmean, w_out.astype(BF16), bias)


def kernel(x, norm_a_g, w_in_a, qn_a_g, kn_a_g, w_out_a, rel_bias, norm_kv_g, w_kv, kn_b_g,
           norm_b_g, w_in_b, qn_b_g, w_out_b):
    B, T, _ = x.shape
    assert norm_a_g.shape[0] == 1 and norm_b_g.shape[0] == 1
    bias = _bias_tiles(rel_bias)
    qT, k2, vT, sg, iqT, iwT, ik = _proj_a(x, norm_a_g[0], w_in_a[0], qn_a_g[0], kn_a_g[0])
    h = _dsa_attention(x, qT, k2, vT, sg, iqT, iwT, ik, w_out_a[0], bias)
    qT, k2, vT, sg, kmean = _proj_b(h, norm_kv_g, w_kv, kn_b_g, norm_b_g[0], w_in_b[0], qn_b_g[0])
    return _moba_attention(h, qT, k2, vT, sg, kmean.reshape(B, T // TQ, KV_WIDTH), w_out_b[0], bias)
```

```python
import functools
import math

import jax
import jax.numpy as jnp
import numpy as np
from jax import lax
from jax.experimental import pallas as pl
from jax.experimental.pallas import tpu as pltpu

D_MODEL = 1024
N_HEADS = 16
HEAD_DIM = 64
N_KV_HEADS = 4
GROUP = 4
ATTN_WIDTH = 1024
KV_WIDTH = 256
IDX_HEADS = 8
IDX_DIM = 64
DSA_TOPK_MAX = 256
MOBA_BLOCK = 256
MOBA_TOPK = 3
REL_BUCKETS = 32
REL_MAX_DIST = 128
EPS = 1e-6

LANES = 128
SUBLANES = 8
HALF = LANES // 2
TQ = 256
TM_A = 512
N_PAIRS = N_HEADS // 2
HB = 8
VT_ROWS = LANES + 2 * SUBLANES
LOG2E = 1.4426950408889634
Q_SCALE = HEAD_DIM ** -0.5 * LOG2E
NEG_BIG = -1e30
VMEM_LIMIT = 60 * 1024 * 1024

F32 = jnp.float32
BF16 = jnp.bfloat16
I32 = jnp.int32


def _bucket_starts():
    max_exact = REL_BUCKETS // 2
    n = np.arange(0, 4096)
    nf = np.maximum(n, 1).astype(np.float64)
    large = max_exact + (np.log(nf / max_exact) / math.log(REL_MAX_DIST / max_exact)
                         * (REL_BUCKETS - max_exact)).astype(np.int64)
    bucket = np.where(n < max_exact, n, np.minimum(large, REL_BUCKETS - 1))
    nf32 = np.maximum(n, 1).astype(np.float32)
    large32 = max_exact + (np.log(nf32 / np.float32(max_exact)) / np.float32(math.log(REL_MAX_DIST / max_exact))
                           * np.float32(REL_BUCKETS - max_exact)).astype(np.int32)
    bucket32 = np.where(n < max_exact, n, np.minimum(large32, REL_BUCKETS - 1))
    assert (bucket == bucket32).all() and (np.diff(bucket) >= 0).all()
    return [int(np.argmax(bucket >= b)) for b in range(REL_BUCKETS)]


BUCKET_STARTS = _bucket_starts()


def _cparams(n_axes):
    return pltpu.CompilerParams(dimension_semantics=("arbitrary",) * n_axes,
                                vmem_limit_bytes=VMEM_LIMIT)


def _dot_t(a, b):
    return lax.dot_general(a, b, (((1,), (1,)), ((), ())), preferred_element_type=F32)


def _dot(a, b):
    return jnp.dot(a, b, preferred_element_type=F32)


def _const_spec(shape):
    nd = len(shape)
    return pl.BlockSpec(shape, lambda *_: (0,) * nd, pipeline_mode=pl.Buffered(1))


def _bias_kernel(rb_ref, out_ref):
    kind = pl.program_id(0)
    h = pl.program_id(1)
    key = lax.broadcasted_iota(I32, (TQ, TQ), 0)
    qry = lax.broadcasted_iota(I32, (TQ, TQ), 1)
    n = qry - key + (1 - kind) * TQ
    last = rb_ref[REL_BUCKETS - 1, h]
    tile = jnp.full((TQ, TQ), last, F32)
    for b in range(REL_BUCKETS - 2, -1, -1):
        tile = jnp.where(n < BUCKET_STARTS[b + 1], rb_ref[b, h], tile)
    tile = (tile - last) * LOG2E
    out_ref[0, 0] = jnp.where(n < 0, -jnp.inf, tile)


def _bias_tiles(rel_bias):
    return pl.pallas_call(
        _bias_kernel,
        grid=(2, N_HEADS),
        in_specs=[pl.BlockSpec(memory_space=pltpu.SMEM)],
        out_specs=pl.BlockSpec((1, 1, TQ, TQ), lambda k, h: (k, h, 0, 0)),
        out_shape=jax.ShapeDtypeStruct((2, N_HEADS, TQ, TQ), F32),
        compiler_params=_cparams(2), name="bias_tiles",
    )(rel_bias)


def _rms_rows(x):
    return lax.rsqrt(jnp.mean(x * x, axis=-1, keepdims=True) + EPS)


def _half_norm(t, gain2):
    lo = lax.broadcasted_iota(I32, t.shape, 1) < HALF
    s = t * t
    s_lo = jnp.sum(jnp.where(lo, s, 0.0), axis=1, keepdims=True)
    s_hi = jnp.sum(jnp.where(lo, 0.0, s), axis=1, keepdims=True)
    r_lo = lax.rsqrt(s_lo * (1.0 / HEAD_DIM) + EPS)
    r_hi = lax.rsqrt(s_hi * (1.0 / HEAD_DIM) + EPS)
    return t * jnp.where(lo, r_lo, r_hi) * gain2


def _in_half(slab, half):
    z = jnp.zeros_like(slab)
    return jnp.concatenate([slab, z] if half == 0 else [z, slab], axis=0)


def _store_qT(qT_ref, yqT, gain_col):
    for h in range(N_HEADS):
        slab = yqT[h * HEAD_DIM:(h + 1) * HEAD_DIM, :]
        r = lax.rsqrt(jnp.sum(slab * slab, axis=0, keepdims=True) * (1.0 / HEAD_DIM) + EPS)
        qn = slab * r * gain_col * Q_SCALE
        qT_ref[0, h] = _in_half(qn, (h // GROUP) % 2).astype(BF16)


def _store_vT(vT_ref, yvT):
    m = yvT.shape[1]
    ones = jnp.ones((SUBLANES, m), F32)
    zeros = jnp.zeros((SUBLANES, m), F32)
    for g in range(N_KV_HEADS):
        slab = yvT[g * HEAD_DIM:(g + 1) * HEAD_DIM, :]
        for e in range(2):
            tail = [ones, zeros] if e == 0 else [zeros, ones]
            tile = jnp.concatenate([_in_half(slab, e)] + tail, axis=0).astype(BF16)
            for cc in range(m // TQ):
                vT_ref[0, 2 * g + e, cc] = tile[:, cc * TQ:(cc + 1) * TQ]


def _silu(y):
    return y * jax.nn.sigmoid(y)


def _proj_out_specs(tm, T):
    nt = T // TQ
    specs = dict(
        qT=(pl.BlockSpec((1, N_HEADS, LANES, tm), lambda b, i: (b, 0, 0, i)),
            lambda B: jax.ShapeDtypeStruct((B, N_HEADS, LANES, T), BF16)),
        k2=(pl.BlockSpec((1, 2, tm, LANES), lambda b, i: (b, 0, i, 0)),
            lambda B: jax.ShapeDtypeStruct((B, 2, T, LANES), BF16)),
        vT=(pl.BlockSpec((1, 2 * N_KV_HEADS, tm // TQ, VT_ROWS, TQ), lambda b, i: (b, 0, i, 0, 0)),
            lambda B: jax.ShapeDtypeStruct((B, 2 * N_KV_HEADS, nt, VT_ROWS, TQ), BF16)),
        sg=(pl.BlockSpec((1, tm, ATTN_WIDTH), lambda b, i: (b, i, 0)),
            lambda B: jax.ShapeDtypeStruct((B, T, ATTN_WIDTH), BF16)),
    )
    return specs


def _proj_a_kernel(x_ref, g_ref, wq_ref, wn_ref, wr_ref, qg_ref, kg_ref,
                   qT_ref, k_ref, vT_ref, sg_ref, iqT_ref, iwT_ref, ik_ref):
    x = x_ref[0]
    h = (x * _rms_rows(x) * g_ref[...]).astype(BF16)
    _store_qT(qT_ref, _dot_t(wq_ref[...], h), qg_ref[...])
    yn = _dot(h, wn_ref[...])
    for m in range(2):
        k_ref[0, m] = _half_norm(yn[:, m * LANES:(m + 1) * LANES], kg_ref[...]).astype(BF16)
    sg_ref[0] = _silu(yn[:, KV_WIDTH:KV_WIDTH + ATTN_WIDTH]).astype(BF16)
    ik_ref[0] = yn[:, KV_WIDTH + ATTN_WIDTH:].astype(BF16)
    yr = _dot_t(wr_ref[...], h)
    _store_vT(vT_ref, yr[:KV_WIDTH])
    for hh in range(IDX_HEADS):
        slab = yr[KV_WIDTH + hh * IDX_DIM:KV_WIDTH + (hh + 1) * IDX_DIM, :]
        iqT_ref[0, hh] = _in_half(slab, 0).astype(BF16)
    iw0 = KV_WIDTH + IDX_HEADS * IDX_DIM
    iwT_ref[0] = yr[iw0:iw0 + IDX_HEADS, :] * (IDX_HEADS ** -0.5 * IDX_DIM ** -0.5)


def _proj_a(x, norm_g, w_in, qn_g, kn_g):
    B, T, _ = x.shape
    c = np.cumsum([0, ATTN_WIDTH, KV_WIDTH, KV_WIDTH, ATTN_WIDTH, IDX_HEADS * IDX_DIM, IDX_HEADS, IDX_DIM])
    part = lambda j: w_in[:, c[j]:c[j + 1]]
    wq_t = part(0).T.astype(BF16)
    wn = jnp.concatenate([part(1), part(3), jnp.pad(part(6), ((0, 0), (0, LANES - IDX_DIM)))],
                         axis=1).astype(BF16)
    wr_t = jnp.concatenate([part(2), part(4), jnp.pad(part(5), ((0, 0), (0, IDX_HEADS)))],
                           axis=1).T.astype(BF16)
    so = _proj_out_specs(TM_A, T)
    outs = [so["qT"], so["k2"], so["vT"], so["sg"],
            (pl.BlockSpec((1, IDX_HEADS, LANES, TM_A), lambda b, i: (b, 0, 0, i)),
             lambda B: jax.ShapeDtypeStruct((B, IDX_HEADS, LANES, T), BF16)),
            (pl.BlockSpec((1, IDX_HEADS, TM_A), lambda b, i: (b, 0, i)),
             lambda B: jax.ShapeDtypeStruct((B, IDX_HEADS, T), F32)),
            (pl.BlockSpec((1, TM_A, LANES), lambda b, i: (b, i, 0)),
             lambda B: jax.ShapeDtypeStruct((B, T, LANES), BF16))]
    return pl.pallas_call(
        _proj_a_kernel,
        grid=(B, T // TM_A),
        in_specs=[pl.BlockSpec((1, TM_A, D_MODEL), lambda b, i: (b, i, 0)),
                  _const_spec((1, D_MODEL)), _const_spec(wq_t.shape), _const_spec(wn.shape),
                  _const_spec(wr_t.shape), _const_spec((HEAD_DIM, 1)), _const_spec((1, LANES))],
        out_specs=[o[0] for o in outs],
        out_shape=[o[1](B) for o in outs],
        compiler_params=_cparams(2), name="proj_a",
    )(x, norm_g.reshape(1, D_MODEL), wq_t, wn, wr_t, qn_g.reshape(HEAD_DIM, 1),
      jnp.concatenate([kn_g, kn_g]).reshape(1, LANES))


def _proj_b_kernel(x_ref, gkv_ref, gb_ref, wk_ref, wv_ref, wq_ref, wg_ref, kg_ref, qg_ref,
                   qT_ref, k_ref, vT_ref, sg_ref, km_ref):
    x = x_ref[0]
    xr = x * _rms_rows(x)
    hkv = (xr * gkv_ref[...]).astype(BF16)
    hb = (xr * gb_ref[...]).astype(BF16)
    yk = _dot(hkv, wk_ref[...])
    means = []
    for m in range(2):
        kn = _half_norm(yk[:, m * LANES:(m + 1) * LANES], kg_ref[...])
        k_ref[0, m] = kn.astype(BF16)
        means.append(jnp.sum(kn, axis=0, keepdims=True) * (1.0 / MOBA_BLOCK))
    km_ref[0, 0] = jnp.concatenate(means, axis=1)
    _store_vT(vT_ref, _dot_t(wv_ref[...], hkv))
    _store_qT(qT_ref, _dot_t(wq_ref[...], hb), qg_ref[...])
    sg_ref[0] = _silu(_dot(hb, wg_ref[...])).astype(BF16)


def _proj_b(x, norm_kv_g, w_kv, kn_g, norm_b_g, w_in_b, qn_g):
    B, T, _ = x.shape
    assert T % MOBA_BLOCK == 0 and TQ == MOBA_BLOCK
    wk = w_kv[:, :KV_WIDTH].astype(BF16)
    wv_t = w_kv[:, KV_WIDTH:].T.astype(BF16)
    wq_t = w_in_b[:, :ATTN_WIDTH].T.astype(BF16)
    wg = w_in_b[:, ATTN_WIDTH:].astype(BF16)
    so = _proj_out_specs(TQ, T)
    outs = [so["qT"], so["k2"], so["vT"], so["sg"],
            (pl.BlockSpec((1, 1, 1, KV_WIDTH), lambda b, i: (b, i, 0, 0)),
             lambda B: jax.ShapeDtypeStruct((B, T // TQ, 1, KV_WIDTH), F32))]
    return pl.pallas_call(
        _proj_b_kernel,
        grid=(B, T // TQ),
        in_specs=[pl.BlockSpec((1, TQ, D_MODEL), lambda b, i: (b, i, 0)),
                  _const_spec((1, D_MODEL)), _const_spec((1, D_MODEL)),
                  _const_spec(wk.shape), _const_spec(wv_t.shape), _const_spec(wq_t.shape),
                  _const_spec(wg.shape), _const_spec((1, LANES)), _const_spec((HEAD_DIM, 1))],
        out_specs=[o[0] for o in outs],
        out_shape=[o[1](B) for o in outs],
        compiler_params=_cparams(2), name="proj_b",
    )(x, norm_kv_g.reshape(1, D_MODEL), norm_b_g.reshape(1, D_MODEL), wk, wv_t, wq_t, wg,
      jnp.concatenate([kn_g, kn_g]).reshape(1, LANES), qn_g.reshape(HEAD_DIM, 1))


def _attend(i, qT_ref, k_ref, vT_ref, bias_ref, mask_fn, s_ref, m_ref, acc_ref, o_ref):
    groups_per_body = HB // GROUP

    def outer(o, carry):
        m_ref[...] = jnp.full(m_ref.shape, NEG_BIG, F32)
        acc_ref[...] = jnp.zeros(acc_ref.shape, F32)

        def scores(c, bias_kind):
            koff = pl.multiple_of(c * TQ, TQ)
            for gi in range(groups_per_body):
                g = o * groups_per_body + gi
                kt = k_ref[0, g // 2, pl.ds(koff, TQ), :]
                mask = mask_fn(c, g, bias_kind == 1)
                for hl in range(GROUP * gi, GROUP * (gi + 1)):
                    hd = o * HB + hl
                    s = _dot(kt, qT_ref[0, hd])
                    if bias_kind is not None:
                        s = s + bias_ref[bias_kind, hd]
                    if mask is not None and mask[0] == "tile":
                        s = s + mask[1]
                    s_ref[c * HB + hl] = s
                    m_cur = jnp.max(s, axis=0, keepdims=True)
                    if mask is not None and mask[0] == "col":
                        m_cur = jnp.where(mask[1], m_cur, NEG_BIG)
                    m_ref[hl] = jnp.maximum(m_ref[hl], m_cur)

        def far_body(c, carry):
            scores(c, None)
            return carry

        lax.fori_loop(0, jnp.maximum(i - 1, 0), far_body, 0)

        @pl.when(i >= 1)
        def _():
            scores(i - 1, 0)

        scores(i, 1)

        def values(c, diag):
            for gi in range(groups_per_body):
                g = o * groups_per_body + gi
                mask = mask_fn(c, g, diag)
                for pj in range(2):
                    lp = 2 * gi + pj
                    pv = None
                    for e in range(2):
                        hl = 2 * lp + e
                        m = m_ref[hl]
                        if mask is not None and mask[0] == "col":
                            m = jnp.where(mask[1], m, jnp.inf)
                        pr = jnp.exp2(s_ref[c * HB + hl] - m).astype(BF16)
                        d = _dot(vT_ref[0, 2 * g + e, c], pr)
                        pv = d if pv is None else pv + d
                    acc_ref[lp] = acc_ref[lp] + pv

        def values_body(c, carry):
            values(c, False)
            return carry

        lax.fori_loop(0, i, values_body, 0)
        values(i, True)
        for lp in range(HB // 2):
            inv0 = 1.0 / acc_ref[lp, LANES:LANES + 1, :]
            inv1 = 1.0 / acc_ref[lp, LANES + SUBLANES:LANES + SUBLANES + 1, :]
            o_ref[o * (HB // 2) + lp] = jnp.concatenate(
                [acc_ref[lp, 0:HALF, :] * inv0, acc_ref[lp, HALF:LANES, :] * inv1], axis=0)
        return carry

    lax.fori_loop(0, N_HEADS // HB, outer, 0)


def _gated_out(o_ref, sg_ref, x_ref, wout_ref, out_ref):
    og = [(o_ref[p].T * sg_ref[0, :, p * LANES:(p + 1) * LANES].astype(F32)).astype(BF16)
          for p in range(N_PAIRS)]
    out_ref[0] = x_ref[0] + _dot(jnp.concatenate(og, axis=1), wout_ref[...])


def _attn_scratch(nt):
    return [pltpu.VMEM((nt * HB, TQ, TQ), F32),
            pltpu.VMEM((HB, 1, TQ), F32),
            pltpu.VMEM((HB // 2, VT_ROWS, TQ), F32),
            pltpu.VMEM((N_PAIRS, LANES, TQ), F32)]


def _attn_specs(T):
    nt = T // TQ
    return dict(
        qT=pl.BlockSpec((1, N_HEADS, LANES, TQ), lambda b, i: (b, 0, 0, i)),
        k2=pl.BlockSpec((1, 2, T, LANES), lambda b, i: (b, 0, 0, 0)),
        vT=pl.BlockSpec((1, 2 * N_KV_HEADS, nt, VT_ROWS, TQ), lambda b, i: (b, 0, 0, 0, 0)),
        row=lambda n: pl.BlockSpec((1, TQ, n), lambda b, i: (b, i, 0)),
        wout=_const_spec((ATTN_WIDTH, D_MODEL)),
        bias=_const_spec((2, N_HEADS, TQ, TQ)),
    )


INT_MIN = -2 ** 31
KEY_NEG_INF = -0x7F800000


def _key_to_float(k):
    bits = jnp.where(k < 0, jnp.int32(INT_MIN) - k, k)
    return lax.bitcast_convert_type(bits, F32)


def _dsa_kernel(nt, topk, qT_ref, sg_ref, x_ref, iqT_ref, iwT_ref, k_ref, vT_ref, ik_ref,
                wout_ref, bias_ref, out_ref, sc_ref, madd_ref, s_ref, m_ref, acc_ref, o_ref):
    i = pl.program_id(1)
    key = lax.broadcasted_iota(I32, (TQ, TQ), 0)
    qry = lax.broadcasted_iota(I32, (TQ, TQ), 1)
    sub = lax.broadcasted_iota(I32, (SUBLANES, TQ), 0)

    iw = iwT_ref[0]

    def score_chunk(c):
        ik = ik_ref[0, pl.ds(pl.multiple_of(c * TQ, TQ), TQ), :]
        acc = jnp.zeros((TQ, TQ), F32)
        for h in range(IDX_HEADS):
            acc = acc + iw[h:h + 1, :] * jnp.maximum(_dot(ik, iqT_ref[0, h]), 0.0)
        return acc

    def score_body(c, carry):
        sc_ref[c] = score_chunk(c)
        return carry

    lax.fori_loop(0, i, score_body, 0)
    sc_ref[i] = jnp.where(key > qry, -jnp.inf, score_chunk(i))

    def count(pred):
        def body(c, tots):
            tots = list(tots)
            for r in range(TQ // SUBLANES):
                s = sc_ref[c, r * SUBLANES:(r + 1) * SUBLANES, :]
                hit = pred(s, c * TQ + r * SUBLANES + sub)
                tots[r % len(tots)] = tots[r % len(tots)] + jnp.where(hit, 1.0, 0.0)
            return tuple(tots)
        zero = jnp.zeros((SUBLANES, TQ), F32)
        t = lax.fori_loop(0, i + 1, body, (zero,) * 4)
        return jnp.sum((t[0] + t[1]) + (t[2] + t[3]), axis=0, keepdims=True)

    def bit_body(b, carry):
        u, cnt_t = carry
        uc = u | lax.shift_left(jnp.int32(1), 31 - b)
        in_range = (uc >= 0) | (uc <= -(1 << 24))
        cand = jnp.broadcast_to(_key_to_float(jnp.int32(KEY_NEG_INF) + uc), (SUBLANES, TQ))
        cnt = count(lambda s, pos: s >= cand)
        ok = in_range & (cnt >= topk)
        return jnp.where(ok, uc, u), jnp.where(ok, cnt, cnt_t)

    u0 = jnp.zeros((1, TQ), I32)
    cnt0 = jnp.full((1, TQ), float(nt * TQ), F32)
    u, cnt_t = lax.fori_loop(0, 32, bit_body, (u0, cnt0))
    thr = _key_to_float(jnp.int32(KEY_NEG_INF) + u)

    excess = (cnt_t > topk) & (thr > -jnp.inf)
    any_excess = jnp.max(jnp.where(excess, 1.0, 0.0)) > 0.0

    @pl.when(jnp.logical_not(any_excess))
    def _():
        def body(c, carry):
            madd_ref[c] = jnp.where(sc_ref[c] >= thr, 0.0, -jnp.inf)
            return carry
        lax.fori_loop(0, i + 1, body, 0)

    @pl.when(any_excess)
    def _():
        thr8 = jnp.broadcast_to(thr, (SUBLANES, TQ))
        need = topk - count(lambda s, pos: s > thr8)

        def pos_body(b, qpos):
            cand = qpos | lax.shift_left(jnp.int32(1), 10 - b)
            f = count(lambda s, pos: (s == thr8) & (pos < cand))
            return jnp.where(f < need, cand, qpos)

        assert nt * TQ <= 2048
        qpos = lax.fori_loop(0, 11, pos_body, jnp.zeros((1, TQ), I32))
        qpos = jnp.where(excess, qpos, jnp.int32(2 ** 30))

        def body(c, carry):
            s = sc_ref[c]
            sel = (s > thr) | ((s == thr) & (c * TQ + key <= qpos))
            madd_ref[c] = jnp.where(sel, 0.0, -jnp.inf)
            return carry
        lax.fori_loop(0, i + 1, body, 0)

    _attend(i, qT_ref, k_ref, vT_ref, bias_ref, lambda c, g, diag: ("tile", madd_ref[c]),
            s_ref, m_ref, acc_ref, o_ref)
    _gated_out(o_ref, sg_ref, x_ref, wout_ref, out_ref)


def _dsa_attention(x, qT, k2, vT, sg, iqT, iwT, ik, w_out, bias):
    B, T, _ = x.shape
    nt = T // TQ
    topk = min(DSA_TOPK_MAX, T // 4)
    sp = _attn_specs(T)
    return pl.pallas_call(
        functools.partial(_dsa_kernel, nt, topk),
        grid=(B, nt),
        in_specs=[sp["qT"], sp["row"](ATTN_WIDTH), sp["row"](D_MODEL),
                  pl.BlockSpec((1, IDX_HEADS, LANES, TQ), lambda b, i: (b, 0, 0, i)),
                  pl.BlockSpec((1, IDX_HEADS, TQ), lambda b, i: (b, 0, i)),
                  sp["k2"], sp["vT"],
                  pl.BlockSpec((1, T, LANES), lambda b, i: (b, 0, 0)),
                  sp["wout"], sp["bias"]],
        out_specs=sp["row"](D_MODEL),
        out_shape=jax.ShapeDtypeStruct((B, T, D_MODEL), F32),
        scratch_shapes=[pltpu.VMEM((nt, TQ, TQ), F32), pltpu.VMEM((nt, TQ, TQ), F32)] + _attn_scratch(nt),
        compiler_params=_cparams(2), name="dsa_attention",
    )(qT, sg, x, iqT, iwT, k2, vT, ik, w_out.astype(BF16), bias)


def _split_bf16(a):
    hi = a.astype(BF16)
    return hi, (a - hi.astype(F32)).astype(BF16)


def _moba_kernel(nt, nsel, qT_ref, sg_ref, x_ref, k_ref, vT_ref, km_ref, wout_ref, bias_ref,
                 out_ref, sel_ref, s_ref, m_ref, acc_ref, o_ref):
    i = pl.program_id(1)
    assert nt <= SUBLANES
    blk = lax.broadcasted_iota(I32, (SUBLANES, TQ), 0)

    km = jnp.concatenate([km_ref[0], jnp.zeros((2 * SUBLANES - nt, KV_WIDTH), F32)], axis=0)
    for g in range(N_KV_HEADS):
        qs = qT_ref[0, GROUP * g].astype(F32)
        for j in range(1, GROUP):
            qs = qs + qT_ref[0, GROUP * g + j].astype(F32)
        q_hi, q_lo = _split_bf16(qs)
        m_hi, m_lo = _split_bf16(km[:, (g // 2) * LANES:(g // 2 + 1) * LANES])
        gs = (_dot(m_hi, q_hi) + (_dot(m_hi, q_lo) + _dot(m_lo, q_hi)))[:SUBLANES]
        rank = jnp.zeros((SUBLANES, TQ), F32)
        for n in range(nt):
            gn = gs[n:n + 1, :]
            beats = (gn > gs) | ((gn == gs) & (n < blk))
            rank = rank + jnp.where(beats, jnp.where(n < i, 1.0, 0.0), 0.0)
        sel_ref[g] = jnp.where((rank < nsel) & (blk < i), 1.0, 0.0)

    def mask_fn(c, g, diag):
        if diag:
            return None
        return ("col", sel_ref[g, pl.ds(c, 1), :] > 0.5)

    _attend(i, qT_ref, k_ref, vT_ref, bias_ref, mask_fn, s_ref, m_ref, acc_ref, o_ref)
    _gated_out(o_ref, sg_ref, x_ref, wout_ref, out_ref)


def _moba_attention(x, qT, k2, vT, sg, kmean, w_out, bias):
    B, T, _ = x.shape
    nt = T // TQ
    nsel = min(MOBA_TOPK, nt - 1)
    sp = _attn_specs(T)
    return pl.pallas_call(
        functools.partial(_moba_kernel, nt, nsel),
        grid=(B, nt),
        in_specs=[sp["qT"], sp["row"](ATTN_WIDTH), sp["row"](D_MODEL), sp["k2"], sp["vT"],
                  pl.BlockSpec((1, nt, KV_WIDTH), lambda b, i: (b, 0, 0)),
                  sp["wout"], sp["bias"]],
        out_specs=sp["row"](D_MODEL),
        out_shape=jax.ShapeDtypeStruct((B, T, D_MODEL), F32),
        scratch_shapes=[pltpu.VMEM((N_KV_HEADS, SUBLANES, TQ), F32)] + _attn_scratch(nt),
        compiler_params=_cparams(2), name="moba_attention",
    )(qT, sg, x, k2, vT, kmean, w_out.astype(BF16), bias)


def kernel(x, norm_a_g, w_in_a, qn_a_g, kn_a_g, w_out_a, rel_bias, norm_kv_g, w_kv, kn_b_g,
           norm_b_g, w_in_b, qn_b_g, w_out_b):
    B, T, _ = x.shape
    assert norm_a_g.shape[0] == 1 and norm_b_g.shape[0] == 1
    bias = _bias_tiles(rel_bias)
    qT, k2, vT, sg, iqT, iwT, ik = _proj_a(x, norm_a_g[0], w_in_a[0], qn_a_g[0], kn_a_g[0])
    h = _dsa_attention(x, qT, k2, vT, sg, iqT, iwT, ik, w_out_a[0], bias)
    qT, k2, vT, sg, kmean = _proj_b(h, norm_kv_g, w_kv, kn_b_g, norm_b_g[0], w_in_b[0], qn_b_g[0])
    return _moba_attention(h, qT, k2, vT, sg, kmean.reshape(B, T // TQ, KV_WIDTH), w_out_b[0], bias)
```

```python
import functools
import math

import jax
import jax.numpy as jnp
import numpy as np
from jax import lax
from jax.experimental import pallas as pl
from jax.experimental.pallas import tpu as pltpu

D_MODEL = 1024
N_HEADS = 16
HEAD_DIM = 64
N_KV_HEADS = 4
GROUP = 4
ATTN_WIDTH = 1024
KV_WIDTH = 256
IDX_HEADS = 8
IDX_DIM = 64
DSA_TOPK_MAX = 256
MOBA_BLOCK = 256
MOBA_TOPK = 3
REL_BUCKETS = 32
REL_MAX_DIST = 128
EPS = 1e-6

LANES = 128
SUBLANES = 8
HALF = LANES // 2
TQ = 256
TM_A = 512
N_PAIRS = N_HEADS // 2
HB = 8
VT_ROWS = LANES + 2 * SUBLANES
LOG2E = 1.4426950408889634
Q_SCALE = HEAD_DIM ** -0.5 * LOG2E
NEG_BIG = -1e30
VMEM_LIMIT = 60 * 1024 * 1024

F32 = jnp.float32
BF16 = jnp.bfloat16
I32 = jnp.int32


def _bucket_starts():
    max_exact = REL_BUCKETS // 2
    n = np.arange(0, 4096)
    nf = np.maximum(n, 1).astype(np.float64)
    large = max_exact + (np.log(nf / max_exact) / math.log(REL_MAX_DIST / max_exact)
                         * (REL_BUCKETS - max_exact)).astype(np.int64)
    bucket = np.where(n < max_exact, n, np.minimum(large, REL_BUCKETS - 1))
    nf32 = np.maximum(n, 1).astype(np.float32)
    large32 = max_exact + (np.log(nf32 / np.float32(max_exact)) / np.float32(math.log(REL_MAX_DIST / max_exact))
                           * np.float32(REL_BUCKETS - max_exact)).astype(np.int32)
    bucket32 = np.where(n < max_exact, n, np.minimum(large32, REL_BUCKETS - 1))
    assert (bucket == bucket32).all() and (np.diff(bucket) >= 0).all()
    return [int(np.argmax(bucket >= b)) for b in range(REL_BUCKETS)]


BUCKET_STARTS = _bucket_starts()


def _cparams(n_axes):
    return pltpu.CompilerParams(dimension_semantics=("arbitrary",) * n_axes,
                                vmem_limit_bytes=VMEM_LIMIT)


def _dot_t(a, b):
    return lax.dot_general(a, b, (((1,), (1,)), ((), ())), preferred_element_type=F32)


def _dot(a, b):
    return jnp.dot(a, b, preferred_element_type=F32)


def _const_spec(shape):
    nd = len(shape)
    return pl.BlockSpec(shape, lambda *_: (0,) * nd, pipeline_mode=pl.Buffered(1))


def _bias_kernel(rb_ref, out_ref):
    kind = pl.program_id(0)
    h = pl.program_id(1)
    key = lax.broadcasted_iota(I32, (TQ, TQ), 0)
    qry = lax.broadcasted_iota(I32, (TQ, TQ), 1)
    n = qry - key + (1 - kind) * TQ
    last = rb_ref[REL_BUCKETS - 1, h]
    tile = jnp.full((TQ, TQ), last, F32)
    for b in range(REL_BUCKETS - 2, -1, -1):
        tile = jnp.where(n < BUCKET_STARTS[b + 1], rb_ref[b, h], tile)
    tile = (tile - last) * LOG2E
    out_ref[0, 0] = jnp.where(n < 0, -jnp.inf, tile)


def _bias_tiles(rel_bias):
    return pl.pallas_call(
        _bias_kernel,
        grid=(2, N_HEADS),
        in_specs=[pl.BlockSpec(memory_space=pltpu.SMEM)],
        out_specs=pl.BlockSpec((1, 1, TQ, TQ), lambda k, h: (k, h, 0, 0)),
        out_shape=jax.ShapeDtypeStruct((2, N_HEADS, TQ, TQ), F32),
        compiler_params=_cparams(2), name="bias_tiles",
    )(rel_bias)


def _rms_rows(x):
    return lax.rsqrt(jnp.mean(x * x, axis=-1, keepdims=True) + EPS)


def _half_norm(t, gain2):
    lo = lax.broadcasted_iota(I32, t.shape, 1) < HALF
    s = t * t
    s_lo = jnp.sum(jnp.where(lo, s, 0.0), axis=1, keepdims=True)
    s_hi = jnp.sum(jnp.where(lo, 0.0, s), axis=1, keepdims=True)
    r_lo = lax.rsqrt(s_lo * (1.0 / HEAD_DIM) + EPS)
    r_hi = lax.rsqrt(s_hi * (1.0 / HEAD_DIM) + EPS)
    return t * jnp.where(lo, r_lo, r_hi) * gain2


def _in_half(slab, half):
    z = jnp.zeros_like(slab)
    return jnp.concatenate([slab, z] if half == 0 else [z, slab], axis=0)


def _store_qT(qT_ref, yqT, gain_col):
    for h in range(N_HEADS):
        slab = yqT[h * HEAD_DIM:(h + 1) * HEAD_DIM, :]
        r = lax.rsqrt(jnp.sum(slab * slab, axis=0, keepdims=True) * (1.0 / HEAD_DIM) + EPS)
        qn = slab * r * gain_col * Q_SCALE
        qT_ref[0, h] = _in_half(qn, (h // GROUP) % 2).astype(BF16)


def _store_vT(vT_ref, yvT):
    m = yvT.shape[1]
    ones = jnp.ones((SUBLANES, m), F32)
    zeros = jnp.zeros((SUBLANES, m), F32)
    for g in range(N_KV_HEADS):
        slab = yvT[g * HEAD_DIM:(g + 1) * HEAD_DIM, :]
        for e in range(2):
            tail = [ones, zeros] if e == 0 else [zeros, ones]
            tile = jnp.concatenate([_in_half(slab, e)] + tail, axis=0).astype(BF16)
            for cc in range(m // TQ):
                vT_ref[0, 2 * g + e, cc] = tile[:, cc * TQ:(cc + 1) * TQ]


def _silu(y):
    return y * jax.nn.sigmoid(y)


def _proj_out_specs(tm, T):
    nt = T // TQ
    specs = dict(
        qT=(pl.BlockSpec((1, N_HEADS, LANES, tm), lambda b, i: (b, 0, 0, i)),
            lambda B: jax.ShapeDtypeStruct((B, N_HEADS, LANES, T), BF16)),
        k2=(pl.BlockSpec((1, 2, tm, LANES), lambda b, i: (b, 0, i, 0)),
            lambda B: jax.ShapeDtypeStruct((B, 2, T, LANES), BF16)),
        vT=(pl.BlockSpec((1, 2 * N_KV_HEADS, tm // TQ, VT_ROWS, TQ), lambda b, i: (b, 0, i, 0, 0)),
            lambda B: jax.ShapeDtypeStruct((B, 2 * N_KV_HEADS, nt, VT_ROWS, TQ), BF16)),
        sg=(pl.BlockSpec((1, tm, ATTN_WIDTH), lambda b, i: (b, i, 0)),
            lambda B: jax.ShapeDtypeStruct((B, T, ATTN_WIDTH), BF16)),
    )
    return specs


def _proj_a_kernel(x_ref, g_ref, wq_ref, wn_ref, wr_ref, qg_ref, kg_ref,
                   qT_ref, k_ref, vT_ref, sg_ref, iqT_ref, iwT_ref, ik_ref):
    x = x_ref[0]
    h = (x * _rms_rows(x) * g_ref[...]).astype(BF16)
    _store_qT(qT_ref, _dot_t(wq_ref[...], h), qg_ref[...])
    yn = _dot(h, wn_ref[...])
    for m in range(2):
        k_ref[0, m] = _half_norm(yn[:, m * LANES:(m + 1) * LANES], kg_ref[...]).astype(BF16)
    sg_ref[0] = _silu(yn[:, KV_WIDTH:KV_WIDTH + ATTN_WIDTH]).astype(BF16)
    ik_ref[0] = yn[:, KV_WIDTH + ATTN_WIDTH:].astype(BF16)
    yr = _dot_t(wr_ref[...], h)
    _store_vT(vT_ref, yr[:KV_WIDTH])
    for hh in range(IDX_HEADS):
        slab = yr[KV_WIDTH + hh * IDX_DIM:KV_WIDTH + (hh + 1) * IDX_DIM, :]
        iqT_ref[0, hh] = _in_half(slab, 0).astype(BF16)
    iw0 = KV_WIDTH + IDX_HEADS * IDX_DIM
    iwT_ref[0] = yr[iw0:iw0 + IDX_HEADS, :] * (IDX_HEADS ** -0.5 * IDX_DIM ** -0.5)


def _proj_a(x, norm_g, w_in, qn_g, kn_g):
    B, T, _ = x.shape
    c = np.cumsum([0, ATTN_WIDTH, KV_WIDTH, KV_WIDTH, ATTN_WIDTH, IDX_HEADS * IDX_DIM, IDX_HEADS, IDX_DIM])
    part = lambda j: w_in[:, c[j]:c[j + 1]]
    wq_t = part(0).T.astype(BF16)
    wn = jnp.concatenate([part(1), part(3), jnp.pad(part(6), ((0, 0), (0, LANES - IDX_DIM)))],
                         axis=1).astype(BF16)
    wr_t = jnp.concatenate([part(2), part(4), jnp.pad(part(5), ((0, 0), (0, IDX_HEADS)))],
                           axis=1).T.astype(BF16)
    so = _proj_out_specs(TM_A, T)
    outs = [so["qT"], so["k2"], so["vT"], so["sg"],
            (pl.BlockSpec((1, IDX_HEADS, LANES, TM_A), lambda b, i: (b, 0, 0, i)),
             lambda B: jax.ShapeDtypeStruct((B, IDX_HEADS, LANES, T), BF16)),
            (pl.BlockSpec((1, IDX_HEADS, TM_A), lambda b, i: (b, 0, i)),
             lambda B: jax.ShapeDtypeStruct((B, IDX_HEADS, T), F32)),
            (pl.BlockSpec((1, TM_A, LANES), lambda b, i: (b, i, 0)),
             lambda B: jax.ShapeDtypeStruct((B, T, LANES), BF16))]
    return pl.pallas_call(
        _proj_a_kernel,
        grid=(B, T // TM_A),
        in_specs=[pl.BlockSpec((1, TM_A, D_MODEL), lambda b, i: (b, i, 0)),
                  _const_spec((1, D_MODEL)), _const_spec(wq_t.shape), _const_spec(wn.shape),
                  _const_spec(wr_t.shape), _const_spec((HEAD_DIM, 1)), _const_spec((1, LANES))],
        out_specs=[o[0] for o in outs],
        out_shape=[o[1](B) for o in outs],
        compiler_params=_cparams(2), name="proj_a",
    )(x, norm_g.reshape(1, D_MODEL), wq_t, wn, wr_t, qn_g.reshape(HEAD_DIM, 1),
      jnp.concatenate([kn_g, kn_g]).reshape(1, LANES))


def _proj_b_kernel(x_ref, gkv_ref, gb_ref, wk_ref, wv_ref, wq_ref, wg_ref, kg_ref, qg_ref,
                   qT_ref, k_ref, vT_ref, sg_ref, km_ref):
    x = x_ref[0]
    xr = x * _rms_rows(x)
    hkv = (xr * gkv_ref[...]).astype(BF16)
    hb = (xr * gb_ref[...]).astype(BF16)
    yk = _dot(hkv, wk_ref[...])
    means = []
    for m in range(2):
        kn = _half_norm(yk[:, m * LANES:(m + 1) * LANES], kg_ref[...])
        k_ref[0, m] = kn.astype(BF16)
        means.append(jnp.sum(kn, axis=0, keepdims=True) * (1.0 / MOBA_BLOCK))
    km_ref[0, 0] = jnp.concatenate(means, axis=1)
    _store_vT(vT_ref, _dot_t(wv_ref[...], hkv))
    _store_qT(qT_ref, _dot_t(wq_ref[...], hb), qg_ref[...])
    sg_ref[0] = _silu(_dot(hb, wg_ref[...])).astype(BF16)


def _proj_b(x, norm_kv_g, w_kv, kn_g, norm_b_g, w_in_b, qn_g):
    B, T, _ = x.shape
    assert T % MOBA_BLOCK == 0 and TQ == MOBA_BLOCK
    wk = w_kv[:, :KV_WIDTH].astype(BF16)
    wv_t = w_kv[:, KV_WIDTH:].T.astype(BF16)
    wq_t = w_in_b[:, :ATTN_WIDTH].T.astype(BF16)
    wg = w_in_b[:, ATTN_WIDTH:].astype(BF16)
    so = _proj_out_specs(TQ, T)
    outs = [so["qT"], so["k2"], so["vT"], so["sg"],
            (pl.BlockSpec((1, 1, 1, KV_WIDTH), lambda b, i: (b, i, 0, 0)),
             lambda B: jax.ShapeDtypeStruct((B, T // TQ, 1, KV_WIDTH), F32))]
    return pl.pallas_call(
        _proj_b_kernel,
        grid=(B, T // TQ),
        in_specs=[pl.BlockSpec((1, TQ, D_MODEL), lambda b, i: (b, i, 0)),
                  _const_spec((1, D_MODEL)), _const_spec((1, D_MODEL)),
                  _const_spec(wk.shape), _const_spec(wv_t.shape), _const_spec(wq_t.shape),
                  _const_spec(wg.shape), _const_spec((1, LANES)), _const_spec((HEAD_DIM, 1))],
        out_specs=[o[0] for o in outs],
        out_shape=[o[1](B) for o in outs],
        compiler_params=_cparams(2), name="proj_b",
    )(x, norm_kv_g.reshape(1, D_MODEL), norm_b_g.reshape(1, D_MODEL), wk, wv_t, wq_t, wg,
      jnp.concatenate([kn_g, kn_g]).reshape(1, LANES), qn_g.reshape(HEAD_DIM, 1))


def _two_per_body(n, fn):
    def body(j, carry):
        fn(2 * j)
        fn(2 * j + 1)
        return carry

    lax.fori_loop(0, lax.shift_right_logical(n, 1), body, 0)

    @pl.when((n & 1) == 1)
    def _():
        fn(n - 1)


def _attend(i, qT_ref, k_ref, vT_ref, bias_ref, mask_fn, s_ref, m_ref, acc_ref, o_ref):
    groups_per_body = HB // GROUP

    def outer(o, carry):
        m_ref[...] = jnp.full(m_ref.shape, NEG_BIG, F32)
        acc_ref[...] = jnp.zeros(acc_ref.shape, F32)

        def scores(c, bias_kind):
            koff = pl.multiple_of(c * TQ, TQ)
            for gi in range(groups_per_body):
                g = o * groups_per_body + gi
                kt = k_ref[0, g // 2, pl.ds(koff, TQ), :]
                mask = mask_fn(c, g, bias_kind == 1)
                for hl in range(GROUP * gi, GROUP * (gi + 1)):
                    hd = o * HB + hl
                    s = _dot(kt, qT_ref[0, hd])
                    if bias_kind is not None:
                        s = s + bias_ref[bias_kind, hd]
                    if mask is not None and mask[0] == "tile":
                        s = s + mask[1]
                    s_ref[c * HB + hl] = s
                    m_cur = jnp.max(s, axis=0, keepdims=True)
                    if mask is not None and mask[0] == "col":
                        m_cur = jnp.where(mask[1], m_cur, NEG_BIG)
                    m_ref[hl] = jnp.maximum(m_ref[hl], m_cur)

        _two_per_body(jnp.maximum(i - 1, 0), lambda c: scores(c, None))

        @pl.when(i >= 1)
        def _():
            scores(i - 1, 0)
            scores(i, 1)

        @pl.when(i == 0)
        def _():
            scores(i, 1)

        def values(c):
            for gi in range(groups_per_body):
                g = o * groups_per_body + gi
                mask = mask_fn(c, g, False)
                for pj in range(2):
                    lp = 2 * gi + pj
                    pv = None
                    for e in range(2):
                        hl = 2 * lp + e
                        m = m_ref[hl]
                        if mask is not None and mask[0] == "col":
                            m = jnp.where(mask[1], m, jnp.inf)
                        pr = jnp.exp2(s_ref[c * HB + hl] - m).astype(BF16)
                        d = _dot(vT_ref[0, 2 * g + e, c], pr)
                        pv = d if pv is None else pv + d
                    acc_ref[lp] = acc_ref[lp] + pv

        _two_per_body(i + 1, values)
        for lp in range(HB // 2):
            inv0 = 1.0 / acc_ref[lp, LANES:LANES + 1, :]
            inv1 = 1.0 / acc_ref[lp, LANES + SUBLANES:LANES + SUBLANES + 1, :]
            o_ref[o * (HB // 2) + lp] = jnp.concatenate(
                [acc_ref[lp, 0:HALF, :] * inv0, acc_ref[lp, HALF:LANES, :] * inv1], axis=0)
        return carry

    lax.fori_loop(0, N_HEADS // HB, outer, 0)


def _gated_out(o_ref, sg_ref, x_ref, wout_ref, out_ref):
    og = [(o_ref[p].T * sg_ref[0, :, p * LANES:(p + 1) * LANES].astype(F32)).astype(BF16)
          for p in range(N_PAIRS)]
    out_ref[0] = x_ref[0] + _dot(jnp.concatenate(og, axis=1), wout_ref[...])


def _attn_scratch(nt):
    return [pltpu.VMEM((nt * HB, TQ, TQ), F32),
            pltpu.VMEM((HB, 1, TQ), F32),
            pltpu.VMEM((HB // 2, VT_ROWS, TQ), F32),
            pltpu.VMEM((N_PAIRS, LANES, TQ), F32)]


def _attn_specs(T):
    nt = T // TQ
    return dict(
        qT=pl.BlockSpec((1, N_HEADS, LANES, TQ), lambda b, i: (b, 0, 0, i)),
        k2=pl.BlockSpec((1, 2, T, LANES), lambda b, i: (b, 0, 0, 0)),
        vT=pl.BlockSpec((1, 2 * N_KV_HEADS, nt, VT_ROWS, TQ), lambda b, i: (b, 0, 0, 0, 0)),
        row=lambda n: pl.BlockSpec((1, TQ, n), lambda b, i: (b, i, 0)),
        wout=_const_spec((ATTN_WIDTH, D_MODEL)),
        bias=_const_spec((2, N_HEADS, TQ, TQ)),
    )


INT_MIN = -2 ** 31
KEY_NEG_INF = -0x7F800000


def _key_to_float(k):
    bits = jnp.where(k < 0, jnp.int32(INT_MIN) - k, k)
    return lax.bitcast_convert_type(bits, F32)


def _floor_bf16(s):
    bits = lax.bitcast_convert_type(s, I32)
    round_up_magnitude = jnp.right_shift(bits, 31) & jnp.int32(0xFFFF)
    bits = (bits + round_up_magnitude) & jnp.int32(-0x10000)
    return lax.bitcast_convert_type(bits, F32).astype(BF16)


def _dsa_kernel(nt, topk, qT_ref, sg_ref, x_ref, iqT_ref, iwT_ref, k_ref, vT_ref, ik_ref,
                wout_ref, bias_ref, out_ref, sc_ref, sc16_ref, madd_ref, s_ref, m_ref, acc_ref,
                o_ref):
    i = pl.program_id(1)
    key = lax.broadcasted_iota(I32, (TQ, TQ), 0)
    qry = lax.broadcasted_iota(I32, (TQ, TQ), 1)
    sub = lax.broadcasted_iota(I32, (SUBLANES, TQ), 0)

    iw = iwT_ref[0]

    def score_chunk(c):
        ik = ik_ref[0, pl.ds(pl.multiple_of(c * TQ, TQ), TQ), :]
        acc = jnp.zeros((TQ, TQ), F32)
        for h in range(IDX_HEADS):
            acc = acc + iw[h:h + 1, :] * jnp.maximum(_dot(ik, iqT_ref[0, h]), 0.0)
        return acc

    def store_scores(c, s):
        sc_ref[c] = s
        sc16_ref[c] = _floor_bf16(s)

    def score_body(c, carry):
        store_scores(c, score_chunk(c))
        return carry

    lax.fori_loop(0, i, score_body, 0)
    store_scores(i, jnp.where(key > qry, -jnp.inf, score_chunk(i)))

    def count(pred):
        def body(c, tots):
            tots = list(tots)
            for r in range(TQ // SUBLANES):
                s = sc_ref[c, r * SUBLANES:(r + 1) * SUBLANES, :]
                hit = pred(s, c * TQ + r * SUBLANES + sub)
                tots[r % len(tots)] = tots[r % len(tots)] + jnp.where(hit, 1.0, 0.0)
            return tuple(tots)
        zero = jnp.zeros((SUBLANES, TQ), F32)
        t = lax.fori_loop(0, i + 1, body, (zero,) * 4)
        return jnp.sum((t[0] + t[1]) + (t[2] + t[3]), axis=0, keepdims=True)

    def count16(cand16):
        rows = 2 * SUBLANES
        one, zero = jnp.ones((rows, TQ), BF16), jnp.zeros((rows, TQ), BF16)

        def body(c, tots):
            tots = list(tots)
            for r in range(TQ // rows):
                s = sc16_ref[c, r * rows:(r + 1) * rows, :]
                tots[r % len(tots)] = tots[r % len(tots)] + jnp.where(s >= cand16, one, zero)
            return tuple(tots)
        assert nt * (TQ // rows) <= 256
        t = lax.fori_loop(0, i + 1, body, (zero,) * 4)
        return jnp.sum(((t[0] + t[1]) + (t[2] + t[3])).astype(F32), axis=0, keepdims=True)

    def bit_body(packed, b, carry):
        u, cnt_t = carry
        uc = u | lax.shift_left(jnp.int32(1), 31 - b)
        in_range = (uc >= 0) | (uc <= -(1 << 24))
        cand = _key_to_float(jnp.int32(KEY_NEG_INF) + uc)
        if packed:
            cnt = count16(jnp.broadcast_to(cand.astype(BF16), (2 * SUBLANES, TQ)))
        else:
            cand8 = jnp.broadcast_to(cand, (SUBLANES, TQ))
            cnt = count(lambda s, pos: s >= cand8)
        ok = in_range & (cnt >= topk)
        return jnp.where(ok, uc, u), jnp.where(ok, cnt, cnt_t)

    u0 = jnp.zeros((1, TQ), I32)
    cnt0 = jnp.full((1, TQ), float(nt * TQ), F32)
    carry = lax.fori_loop(0, 16, functools.partial(bit_body, True), (u0, cnt0))
    u, cnt_t = lax.fori_loop(16, 32, functools.partial(bit_body, False), carry)
    thr = _key_to_float(jnp.int32(KEY_NEG_INF) + u)

    excess = (cnt_t > topk) & (thr > -jnp.inf)
    any_excess = jnp.max(jnp.where(excess, 1.0, 0.0)) > 0.0

    @pl.when(jnp.logical_not(any_excess))
    def _():
        def body(c, carry):
            madd_ref[c] = jnp.where(sc_ref[c] >= thr, 0.0, -jnp.inf)
            return carry
        lax.fori_loop(0, i + 1, body, 0)

    @pl.when(any_excess)
    def _():
        thr8 = jnp.broadcast_to(thr, (SUBLANES, TQ))
        need = topk - count(lambda s, pos: s > thr8)

        def pos_body(b, qpos):
            cand = qpos | lax.shift_left(jnp.int32(1), 10 - b)
            f = count(lambda s, pos: (s == thr8) & (pos < cand))
            return jnp.where(f < need, cand, qpos)

        assert nt * TQ <= 2048
        qpos = lax.fori_loop(0, 11, pos_body, jnp.zeros((1, TQ), I32))
        qpos = jnp.where(excess, qpos, jnp.int32(2 ** 30))

        def body(c, carry):
            s = sc_ref[c]
            sel = (s > thr) | ((s == thr) & (c * TQ + key <= qpos))
            madd_ref[c] = jnp.where(sel, 0.0, -jnp.inf)
            return carry
        lax.fori_loop(0, i + 1, body, 0)

    _attend(i, qT_ref, k_ref, vT_ref, bias_ref, lambda c, g, diag: ("tile", madd_ref[c]),
            s_ref, m_ref, acc_ref, o_ref)
    _gated_out(o_ref, sg_ref, x_ref, wout_ref, out_ref)


def _dsa_attention(x, qT, k2, vT, sg, iqT, iwT, ik, w_out, bias):
    B, T, _ = x.shape
    nt = T // TQ
    topk = min(DSA_TOPK_MAX, T // 4)
    sp = _attn_specs(T)
    return pl.pallas_call(
        functools.partial(_dsa_kernel, nt, topk),
        grid=(B, nt),
        in_specs=[sp["qT"], sp["row"](ATTN_WIDTH), sp["row"](D_MODEL),
                  pl.BlockSpec((1, IDX_HEADS, LANES, TQ), lambda b, i: (b, 0, 0, i)),
                  pl.BlockSpec((1, IDX_HEADS, TQ), lambda b, i: (b, 0, i)),
                  sp["k2"], sp["vT"],
                  pl.BlockSpec((1, T, LANES), lambda b, i: (b, 0, 0)),
                  sp["wout"], sp["bias"]],
        out_specs=sp["row"](D_MODEL),
        out_shape=jax.ShapeDtypeStruct((B, T, D_MODEL), F32),
        scratch_shapes=[pltpu.VMEM((nt, TQ, TQ), F32), pltpu.VMEM((nt, TQ, TQ), BF16),
                        pltpu.VMEM((nt, TQ, TQ), F32)] + _attn_scratch(nt),
        compiler_params=_cparams(2), name="dsa_attention",
    )(qT, sg, x, iqT, iwT, k2, vT, ik, w_out.astype(BF16), bias)


def _split_bf16(a):
    hi = a.astype(BF16)
    return hi, (a - hi.astype(F32)).astype(BF16)


def _moba_kernel(nt, nsel, qT_ref, sg_ref, x_ref, k_ref, vT_ref, km_ref, wout_ref, bias_ref,
                 out_ref, sel_ref, s_ref, m_ref, acc_ref, o_ref):
    i = pl.program_id(1)
    assert nt <= SUBLANES
    blk = lax.broadcasted_iota(I32, (SUBLANES, TQ), 0)

    km = jnp.concatenate([km_ref[0], jnp.zeros((2 * SUBLANES - nt, KV_WIDTH), F32)], axis=0)
    for g in range(N_KV_HEADS):
        qs = qT_ref[0, GROUP * g].astype(F32)
        for j in range(1, GROUP):
            qs = qs + qT_ref[0, GROUP * g + j].astype(F32)
        q_hi, q_lo = _split_bf16(qs)
        m_hi, m_lo = _split_bf16(km[:, (g // 2) * LANES:(g // 2 + 1) * LANES])
        gs = (_dot(m_hi, q_hi) + (_dot(m_hi, q_lo) + _dot(m_lo, q_hi)))[:SUBLANES]
        rank = jnp.zeros((SUBLANES, TQ), F32)
        for n in range(nt):
            gn = gs[n:n + 1, :]
            beats = (gn > gs) | ((gn == gs) & (n < blk))
            rank = rank + jnp.where(beats, jnp.where(n < i, 1.0, 0.0), 0.0)
        sel_ref[g] = jnp.where(((rank < nsel) & (blk < i)) | (blk == i), 1.0, 0.0)

    def mask_fn(c, g, diag):
        if diag:
            return None
        return ("col", sel_ref[g, pl.ds(c, 1), :] > 0.5)

    _attend(i, qT_ref, k_ref, vT_ref, bias_ref, mask_fn, s_ref, m_ref, acc_ref, o_ref)
    _gated_out(o_ref, sg_ref, x_ref, wout_ref, out_ref)


def _moba_attention(x, qT, k2, vT, sg, kmean, w_out, bias):
    B, T, _ = x.shape
    nt = T // TQ
    nsel = min(MOBA_TOPK, nt - 1)
    sp = _attn_specs(T)
    return pl.pallas_call(
        functools.partial(_moba_kernel, nt, nsel),
        grid=(B, nt),
        in_specs=[sp["qT"], sp["row"](ATTN_WIDTH), sp["row"](D_MODEL), sp["k2"], sp["vT"],
                  pl.BlockSpec((1, nt, KV_WIDTH), lambda b, i: (b, 0, 0)),
                  sp["wout"], sp["bias"]],
        out_specs=sp["row"](D_MODEL),
        out_shape=jax.ShapeDtypeStruct((B, T, D_MODEL), F32),
        scratch_shapes=[pltpu.VMEM((N_KV_HEADS, SUBLANES, TQ), F32)] + _attn_scratch(nt),
        compiler_params=_cparams(2), name="moba_attention",
    )(qT, sg, x, k2, vT, kmean, w_out.astype(BF16), bias)


def kernel(x, norm_a_g, w_in_a, qn_a_g, kn_a_g, w_out_a, rel_bias, norm_kv_g, w_kv, kn_b_g,
           norm_b_g, w_in_b, qn_b_g, w_out_b):
    B, T, _ = x.shape
    assert norm_a_g.shape[0] == 1 and norm_b_g.shape[0] == 1
    bias = _bias_tiles(rel_bias)
    qT, k2, vT, sg, iqT, iwT, ik = _proj_a(x, norm_a_g[0], w_in_a[0], qn_a_g[0], kn_a_g[0])
    h = _dsa_attention(x, qT, k2, vT, sg, iqT, iwT, ik, w_out_a[0], bias)
    qT, k2, vT, sg, kmean = _proj_b(h, norm_kv_g, w_kv, kn_b_g, norm_b_g[0], w_in_b[0], qn_b_g[0])
    return _moba_attention(h, qT, k2, vT, sg, kmean.reshape(B, T // TQ, KV_WIDTH), w_out_b[0], bias)
```

```python
import functools
import math

import jax
import jax.numpy as jnp
import numpy as np
from jax import lax
from jax.experimental import pallas as pl
from jax.experimental.pallas import tpu as pltpu

D_MODEL = 1024
N_HEADS = 16
HEAD_DIM = 64
N_KV_HEADS = 4
GROUP = 4
ATTN_WIDTH = 1024
KV_WIDTH = 256
IDX_HEADS = 8
IDX_DIM = 64
DSA_TOPK_MAX = 256
MOBA_BLOCK = 256
MOBA_TOPK = 3
REL_BUCKETS = 32
REL_MAX_DIST = 128
EPS = 1e-6

LANES = 128
SUBLANES = 8
HALF = LANES // 2
TQ = 256
TM_A = 512
N_PAIRS = N_HEADS // 2
HB = 8
VT_ROWS = LANES + 2 * SUBLANES
LOG2E = 1.4426950408889634
Q_SCALE = HEAD_DIM ** -0.5 * LOG2E
NEG_BIG = -1e30
VMEM_LIMIT = 60 * 1024 * 1024

F32 = jnp.float32
BF16 = jnp.bfloat16
I32 = jnp.int32


def _bucket_starts():
    max_exact = REL_BUCKETS // 2
    n = np.arange(0, 4096)
    nf = np.maximum(n, 1).astype(np.float64)
    large = max_exact + (np.log(nf / max_exact) / math.log(REL_MAX_DIST / max_exact)
                         * (REL_BUCKETS - max_exact)).astype(np.int64)
    bucket = np.where(n < max_exact, n, np.minimum(large, REL_BUCKETS - 1))
    nf32 = np.maximum(n, 1).astype(np.float32)
    large32 = max_exact + (np.log(nf32 / np.float32(max_exact)) / np.float32(math.log(REL_MAX_DIST / max_exact))
                           * np.float32(REL_BUCKETS - max_exact)).astype(np.int32)
    bucket32 = np.where(n < max_exact, n, np.minimum(large32, REL_BUCKETS - 1))
    assert (bucket == bucket32).all() and (np.diff(bucket) >= 0).all()
    return [int(np.argmax(bucket >= b)) for b in range(REL_BUCKETS)]


BUCKET_STARTS = _bucket_starts()


def _cparams(n_axes):
    return pltpu.CompilerParams(dimension_semantics=("arbitrary",) * n_axes,
                                vmem_limit_bytes=VMEM_LIMIT)


def _dot_t(a, b):
    return lax.dot_general(a, b, (((1,), (1,)), ((), ())), preferred_element_type=F32)


def _dot(a, b):
    return jnp.dot(a, b, preferred_element_type=F32)


def _const_spec(shape):
    nd = len(shape)
    return pl.BlockSpec(shape, lambda *_: (0,) * nd, pipeline_mode=pl.Buffered(1))


def _bias_kernel(rb_ref, out_ref):
    kind = pl.program_id(0)
    h = pl.program_id(1)
    key = lax.broadcasted_iota(I32, (TQ, TQ), 0)
    qry = lax.broadcasted_iota(I32, (TQ, TQ), 1)
    n = qry - key + (1 - kind) * TQ
    last = rb_ref[REL_BUCKETS - 1, h]
    tile = jnp.full((TQ, TQ), last, F32)
    for b in range(REL_BUCKETS - 2, -1, -1):
        tile = jnp.where(n < BUCKET_STARTS[b + 1], rb_ref[b, h], tile)
    tile = (tile - last) * LOG2E
    out_ref[0, 0] = jnp.where(n < 0, -jnp.inf, tile)


def _bias_tiles(rel_bias):
    return pl.pallas_call(
        _bias_kernel,
        grid=(2, N_HEADS),
        in_specs=[pl.BlockSpec(memory_space=pltpu.SMEM)],
        out_specs=pl.BlockSpec((1, 1, TQ, TQ), lambda k, h: (k, h, 0, 0)),
        out_shape=jax.ShapeDtypeStruct((2, N_HEADS, TQ, TQ), F32),
        compiler_params=_cparams(2), name="bias_tiles",
    )(rel_bias)


def _rms_rows(x):
    return lax.rsqrt(jnp.mean(x * x, axis=-1, keepdims=True) + EPS)


def _half_norm(t, gain2):
    lo = lax.broadcasted_iota(I32, t.shape, 1) < HALF
    s = t * t
    s_lo = jnp.sum(jnp.where(lo, s, 0.0), axis=1, keepdims=True)
    s_hi = jnp.sum(jnp.where(lo, 0.0, s), axis=1, keepdims=True)
    r_lo = lax.rsqrt(s_lo * (1.0 / HEAD_DIM) + EPS)
    r_hi = lax.rsqrt(s_hi * (1.0 / HEAD_DIM) + EPS)
    return t * jnp.where(lo, r_lo, r_hi) * gain2


def _in_half(slab, half):
    z = jnp.zeros_like(slab)
    return jnp.concatenate([slab, z] if half == 0 else [z, slab], axis=0)


def _store_qT(qT_ref, yqT, gain_col):
    for h in range(N_HEADS):
        slab = yqT[h * HEAD_DIM:(h + 1) * HEAD_DIM, :]
        r = lax.rsqrt(jnp.sum(slab * slab, axis=0, keepdims=True) * (1.0 / HEAD_DIM) + EPS)
        qn = slab * r * gain_col * Q_SCALE
        qT_ref[0, h] = _in_half(qn, (h // GROUP) % 2).astype(BF16)


def _store_vT(vT_ref, yvT):
    m = yvT.shape[1]
    ones = jnp.ones((SUBLANES, m), F32)
    zeros = jnp.zeros((SUBLANES, m), F32)
    for g in range(N_KV_HEADS):
        slab = yvT[g * HEAD_DIM:(g + 1) * HEAD_DIM, :]
        for e in range(2):
            tail = [ones, zeros] if e == 0 else [zeros, ones]
            tile = jnp.concatenate([_in_half(slab, e)] + tail, axis=0).astype(BF16)
            for cc in range(m // TQ):
                vT_ref[0, 2 * g + e, cc] = tile[:, cc * TQ:(cc + 1) * TQ]


def _silu(y):
    return y * jax.nn.sigmoid(y)


def _proj_out_specs(tm, T):
    nt = T // TQ
    specs = dict(
        qT=(pl.BlockSpec((1, N_HEADS, LANES, tm), lambda b, i: (b, 0, 0, i)),
            lambda B: jax.ShapeDtypeStruct((B, N_HEADS, LANES, T), BF16)),
        k2=(pl.BlockSpec((1, 2, tm, LANES), lambda b, i: (b, 0, i, 0)),
            lambda B: jax.ShapeDtypeStruct((B, 2, T, LANES), BF16)),
        vT=(pl.BlockSpec((1, 2 * N_KV_HEADS, tm // TQ, VT_ROWS, TQ), lambda b, i: (b, 0, i, 0, 0)),
            lambda B: jax.ShapeDtypeStruct((B, 2 * N_KV_HEADS, nt, VT_ROWS, TQ), BF16)),
        sg=(pl.BlockSpec((1, tm, ATTN_WIDTH), lambda b, i: (b, i, 0)),
            lambda B: jax.ShapeDtypeStruct((B, T, ATTN_WIDTH), BF16)),
    )
    return specs


def _proj_a_kernel(x_ref, g_ref, wq_ref, wn_ref, wr_ref, qg_ref, kg_ref,
                   qT_ref, k_ref, vT_ref, sg_ref, iqT_ref, iwT_ref, ik_ref):
    x = x_ref[0]
    h = (x * _rms_rows(x) * g_ref[...]).astype(BF16)
    _store_qT(qT_ref, _dot_t(wq_ref[...], h), qg_ref[...])
    yn = _dot(h, wn_ref[...])
    for m in range(2):
        k_ref[0, m] = _half_norm(yn[:, m * LANES:(m + 1) * LANES], kg_ref[...]).astype(BF16)
    sg_ref[0] = _silu(yn[:, KV_WIDTH:KV_WIDTH + ATTN_WIDTH]).astype(BF16)
    ik_ref[0] = yn[:, KV_WIDTH + ATTN_WIDTH:].astype(BF16)
    yr = _dot_t(wr_ref[...], h)
    _store_vT(vT_ref, yr[:KV_WIDTH])
    for hh in range(IDX_HEADS):
        slab = yr[KV_WIDTH + hh * IDX_DIM:KV_WIDTH + (hh + 1) * IDX_DIM, :]
        iqT_ref[0, hh] = _in_half(slab, 0).astype(BF16)
    iw0 = KV_WIDTH + IDX_HEADS * IDX_DIM
    iwT_ref[0] = yr[iw0:iw0 + IDX_HEADS, :] * (IDX_HEADS ** -0.5 * IDX_DIM ** -0.5)


def _proj_a(x, norm_g, w_in, qn_g, kn_g):
    B, T, _ = x.shape
    c = np.cumsum([0, ATTN_WIDTH, KV_WIDTH, KV_WIDTH, ATTN_WIDTH, IDX_HEADS * IDX_DIM, IDX_HEADS, IDX_DIM])
    part = lambda j: w_in[:, c[j]:c[j + 1]]
    wq_t = part(0).T.astype(BF16)
    wn = jnp.concatenate([part(1), part(3), jnp.pad(part(6), ((0, 0), (0, LANES - IDX_DIM)))],
                         axis=1).astype(BF16)
    wr_t = jnp.concatenate([part(2), part(4), jnp.pad(part(5), ((0, 0), (0, IDX_HEADS)))],
                           axis=1).T.astype(BF16)
    so = _proj_out_specs(TM_A, T)
    outs = [so["qT"], so["k2"], so["vT"], so["sg"],
            (pl.BlockSpec((1, IDX_HEADS, LANES, TM_A), lambda b, i: (b, 0, 0, i)),
             lambda B: jax.ShapeDtypeStruct((B, IDX_HEADS, LANES, T), BF16)),
            (pl.BlockSpec((1, IDX_HEADS, TM_A), lambda b, i: (b, 0, i)),
             lambda B: jax.ShapeDtypeStruct((B, IDX_HEADS, T), F32)),
            (pl.BlockSpec((1, TM_A, LANES), lambda b, i: (b, i, 0)),
             lambda B: jax.ShapeDtypeStruct((B, T, LANES), BF16))]
    return pl.pallas_call(
        _proj_a_kernel,
        grid=(B, T // TM_A),
        in_specs=[pl.BlockSpec((1, TM_A, D_MODEL), lambda b, i: (b, i, 0)),
                  _const_spec((1, D_MODEL)), _const_spec(wq_t.shape), _const_spec(wn.shape),
                  _const_spec(wr_t.shape), _const_spec((HEAD_DIM, 1)), _const_spec((1, LANES))],
        out_specs=[o[0] for o in outs],
        out_shape=[o[1](B) for o in outs],
        compiler_params=_cparams(2), name="proj_a",
    )(x, norm_g.reshape(1, D_MODEL), wq_t, wn, wr_t, qn_g.reshape(HEAD_DIM, 1),
      jnp.concatenate([kn_g, kn_g]).reshape(1, LANES))


def _proj_b_kernel(x_ref, gkv_ref, gb_ref, wk_ref, wv_ref, wq_ref, wg_ref, kg_ref, qg_ref,
                   qT_ref, k_ref, vT_ref, sg_ref, km_ref):
    x = x_ref[0]
    xr = x * _rms_rows(x)
    hkv = (xr * gkv_ref[...]).astype(BF16)
    hb = (xr * gb_ref[...]).astype(BF16)
    yk = _dot(hkv, wk_ref[...])
    means = []
    for m in range(2):
        kn = _half_norm(yk[:, m * LANES:(m + 1) * LANES], kg_ref[...])
        k_ref[0, m] = kn.astype(BF16)
        means.append(jnp.sum(kn, axis=0, keepdims=True) * (1.0 / MOBA_BLOCK))
    km_ref[0, 0] = jnp.concatenate(means, axis=1)
    _store_vT(vT_ref, _dot_t(wv_ref[...], hkv))
    _store_qT(qT_ref, _dot_t(wq_ref[...], hb), qg_ref[...])
    sg_ref[0] = _silu(_dot(hb, wg_ref[...])).astype(BF16)


def _proj_b(x, norm_kv_g, w_kv, kn_g, norm_b_g, w_in_b, qn_g):
    B, T, _ = x.shape
    assert T % MOBA_BLOCK == 0 and TQ == MOBA_BLOCK
    wk = w_kv[:, :KV_WIDTH].astype(BF16)
    wv_t = w_kv[:, KV_WIDTH:].T.astype(BF16)
    wq_t = w_in_b[:, :ATTN_WIDTH].T.astype(BF16)
    wg = w_in_b[:, ATTN_WIDTH:].astype(BF16)
    so = _proj_out_specs(TQ, T)
    outs = [so["qT"], so["k2"], so["vT"], so["sg"],
            (pl.BlockSpec((1, 1, 1, KV_WIDTH), lambda b, i: (b, i, 0, 0)),
             lambda B: jax.ShapeDtypeStruct((B, T // TQ, 1, KV_WIDTH), F32))]
    return pl.pallas_call(
        _proj_b_kernel,
        grid=(B, T // TQ),
        in_specs=[pl.BlockSpec((1, TQ, D_MODEL), lambda b, i: (b, i, 0)),
                  _const_spec((1, D_MODEL)), _const_spec((1, D_MODEL)),
                  _const_spec(wk.shape), _const_spec(wv_t.shape), _const_spec(wq_t.shape),
                  _const_spec(wg.shape), _const_spec((1, LANES)), _const_spec((HEAD_DIM, 1))],
        out_specs=[o[0] for o in outs],
        out_shape=[o[1](B) for o in outs],
        compiler_params=_cparams(2), name="proj_b",
    )(x, norm_kv_g.reshape(1, D_MODEL), norm_b_g.reshape(1, D_MODEL), wk, wv_t, wq_t, wg,
      jnp.concatenate([kn_g, kn_g]).reshape(1, LANES), qn_g.reshape(HEAD_DIM, 1))


def _two_per_body(n, fn):
    def body(j, carry):
        fn(2 * j, 2 * j + 1)
        return carry

    lax.fori_loop(0, lax.shift_right_logical(n, 1), body, 0)

    @pl.when((n & 1) == 1)
    def _():
        fn(n - 1)


def _attend(i, nt, qT_ref, k_ref, vT_ref, bias_ref, mask_fn, s_ref, m_ref, acc_ref):
    n_groups = N_HEADS // HB
    kv_per_group = HB // GROUP
    m_ref[...] = jnp.full(m_ref.shape, NEG_BIG, F32)
    acc_ref[...] = jnp.zeros(acc_ref.shape, F32)

    def slot(o, c):
        return c if o % 2 == 0 else jnp.where(c >= 2, c - 2, nt + c)

    def scores(o, c, bias_kind):
        koff = pl.multiple_of(c * TQ, TQ)
        base = slot(o, c) * HB
        for gi in range(kv_per_group):
            g = o * kv_per_group + gi
            kt = k_ref[0, g // 2, pl.ds(koff, TQ), :]
            mask = mask_fn(c, g, bias_kind == 1)
            for hl in range(GROUP * gi, GROUP * (gi + 1)):
                hd = o * HB + hl
                s = _dot(kt, qT_ref[0, hd])
                if bias_kind is not None:
                    s = s + bias_ref[bias_kind, hd]
                if mask is not None and mask[0] == "tile":
                    s = s + mask[1]
                s_ref[base + hl] = s
                m_cur = jnp.max(s, axis=0, keepdims=True)
                if mask is not None and mask[0] == "col":
                    m_cur = jnp.where(mask[1], m_cur, NEG_BIG)
                m_ref[hd] = jnp.maximum(m_ref[hd], m_cur)

    def values(o, c):
        base = slot(o, c) * HB
        for gi in range(kv_per_group):
            g = o * kv_per_group + gi
            mask = mask_fn(c, g, False)
            for pj in range(2):
                pair = 2 * g + pj
                pv = None
                for e in range(2):
                    hd = 2 * pair + e
                    m = m_ref[hd]
                    if mask is not None and mask[0] == "col":
                        m = jnp.where(mask[1], m, jnp.inf)
                    pr = jnp.exp2(s_ref[base + hd - o * HB] - m).astype(BF16)
                    d = _dot(vT_ref[0, 2 * g + e, c], pr)
                    pv = d if pv is None else pv + d
                acc_ref[pair] = acc_ref[pair] + pv

    far = jnp.maximum(i - 1, 0)
    for o in range(n_groups + 1):
        def far_chunks(*cs, o=o):
            for c in cs:
                if o >= 1:
                    values(o - 1, c)
            for c in cs:
                if o < n_groups:
                    scores(o, c, None)

        _two_per_body(far, far_chunks)

        @pl.when(i >= 1)
        def _(o=o):
            if o >= 1:
                values(o - 1, i - 1)
                values(o - 1, i)
            if o < n_groups:
                scores(o, i - 1, 0)
                scores(o, i, 1)

        @pl.when(i == 0)
        def _(o=o):
            if o >= 1:
                values(o - 1, i)
            if o < n_groups:
                scores(o, i, 1)


def _gated_out(acc_ref, sg_ref, x_ref, wout_ref, out_ref):
    og = []
    for p in range(N_PAIRS):
        acc = acc_ref[p]
        inv0 = 1.0 / acc[LANES:LANES + 1, :]
        inv1 = 1.0 / acc[LANES + SUBLANES:LANES + SUBLANES + 1, :]
        o_t = jnp.concatenate([acc[0:HALF, :] * inv0, acc[HALF:LANES, :] * inv1], axis=0)
        og.append((o_t.T * sg_ref[0, :, p * LANES:(p + 1) * LANES].astype(F32)).astype(BF16))
    out_ref[0] = x_ref[0] + _dot(jnp.concatenate(og, axis=1), wout_ref[...])


def _attn_scratch(nt):
    return [pltpu.VMEM(((nt + 2) * HB, TQ, TQ), F32),
            pltpu.VMEM((N_HEADS, 1, TQ), F32),
            pltpu.VMEM((N_PAIRS, VT_ROWS, TQ), F32)]


def _attn_specs(T):
    nt = T // TQ
    return dict(
        qT=pl.BlockSpec((1, N_HEADS, LANES, TQ), lambda b, i: (b, 0, 0, i)),
        k2=pl.BlockSpec((1, 2, T, LANES), lambda b, i: (b, 0, 0, 0)),
        vT=pl.BlockSpec((1, 2 * N_KV_HEADS, nt, VT_ROWS, TQ), lambda b, i: (b, 0, 0, 0, 0)),
        row=lambda n: pl.BlockSpec((1, TQ, n), lambda b, i: (b, i, 0)),
        wout=_const_spec((ATTN_WIDTH, D_MODEL)),
        bias=_const_spec((2, N_HEADS, TQ, TQ)),
    )


INT_MIN = -2 ** 31
KEY_NEG_INF = -0x7F800000


def _key_to_float(k):
    bits = jnp.where(k < 0, jnp.int32(INT_MIN) - k, k)
    return lax.bitcast_convert_type(bits, F32)


def _floor_bf16(s):
    bits = lax.bitcast_convert_type(s, I32)
    round_up_magnitude = jnp.right_shift(bits, 31) & jnp.int32(0xFFFF)
    bits = (bits + round_up_magnitude) & jnp.int32(-0x10000)
    return lax.bitcast_convert_type(bits, F32).astype(BF16)


def _dsa_kernel(nt, topk, qT_ref, sg_ref, x_ref, iqT_ref, iwT_ref, k_ref, vT_ref, ik_ref,
                wout_ref, bias_ref, out_ref, sc16_ref, madd_ref, s_ref, m_ref, acc_ref):
    i = pl.program_id(1)
    key = lax.broadcasted_iota(I32, (TQ, TQ), 0)
    qry = lax.broadcasted_iota(I32, (TQ, TQ), 1)
    sub = lax.broadcasted_iota(I32, (SUBLANES, TQ), 0)
    assert nt <= 2 * HB
    sc_ref = s_ref.at[pl.ds(nt * HB, nt)]

    iw = iwT_ref[0]

    def score_chunk(c):
        ik = ik_ref[0, pl.ds(pl.multiple_of(c * TQ, TQ), TQ), :]
        acc = jnp.zeros((TQ, TQ), F32)
        for h in range(IDX_HEADS):
            acc = acc + iw[h:h + 1, :] * jnp.maximum(_dot(ik, iqT_ref[0, h]), 0.0)
        return acc

    def store_scores(c, s):
        sc_ref[c] = s
        sc16_ref[c] = _floor_bf16(s)

    _two_per_body(i, lambda *cs: [store_scores(c, score_chunk(c)) for c in cs])
    store_scores(i, jnp.where(key > qry, -jnp.inf, score_chunk(i)))

    def count(pred):
        def body(c, tots):
            tots = list(tots)
            for r in range(TQ // SUBLANES):
                s = sc_ref[c, r * SUBLANES:(r + 1) * SUBLANES, :]
                hit = pred(s, c * TQ + r * SUBLANES + sub)
                tots[r % len(tots)] = tots[r % len(tots)] + jnp.where(hit, 1.0, 0.0)
            return tuple(tots)
        zero = jnp.zeros((SUBLANES, TQ), F32)
        t = lax.fori_loop(0, i + 1, body, (zero,) * 4)
        return jnp.sum((t[0] + t[1]) + (t[2] + t[3]), axis=0, keepdims=True)

    def count16(cand16):
        rows = 2 * SUBLANES
        one, zero = jnp.ones((rows, TQ), BF16), jnp.zeros((rows, TQ), BF16)

        def body(c, tots):
            tots = list(tots)
            for r in range(TQ // rows):
                s = sc16_ref[c, r * rows:(r + 1) * rows, :]
                tots[r % len(tots)] = tots[r % len(tots)] + jnp.where(s >= cand16, one, zero)
            return tuple(tots)
        assert nt * (TQ // rows) <= 256
        t = lax.fori_loop(0, i + 1, body, (zero,) * 4)
        return jnp.sum(((t[0] + t[1]) + (t[2] + t[3])).astype(F32), axis=0, keepdims=True)

    def bit_body(packed, b, carry):
        u, cnt_t = carry
        uc = u | lax.shift_left(jnp.int32(1), 31 - b)
        in_range = (uc >= 0) | (uc <= -(1 << 24))
        cand = _key_to_float(jnp.int32(KEY_NEG_INF) + uc)
        if packed:
            cnt = count16(jnp.broadcast_to(cand.astype(BF16), (2 * SUBLANES, TQ)))
        else:
            cand8 = jnp.broadcast_to(cand, (SUBLANES, TQ))
            cnt = count(lambda s, pos: s >= cand8)
        ok = in_range & (cnt >= topk)
        return jnp.where(ok, uc, u), jnp.where(ok, cnt, cnt_t)

    u0 = jnp.zeros((1, TQ), I32)
    cnt0 = jnp.full((1, TQ), float(nt * TQ), F32)
    carry = lax.fori_loop(0, 16, functools.partial(bit_body, True), (u0, cnt0))
    u, cnt_t = lax.fori_loop(16, 32, functools.partial(bit_body, False), carry)
    thr = _key_to_float(jnp.int32(KEY_NEG_INF) + u)

    excess = (cnt_t > topk) & (thr > -jnp.inf)
    any_excess = jnp.max(jnp.where(excess, 1.0, 0.0)) > 0.0

    @pl.when(jnp.logical_not(any_excess))
    def _():
        def body(c, carry):
            madd_ref[c] = jnp.where(sc_ref[c] >= thr, 0.0, -jnp.inf)
            return carry
        lax.fori_loop(0, i + 1, body, 0)

    @pl.when(any_excess)
    def _():
        thr8 = jnp.broadcast_to(thr, (SUBLANES, TQ))
        need = topk - count(lambda s, pos: s > thr8)

        def pos_body(b, qpos):
            cand = qpos | lax.shift_left(jnp.int32(1), 10 - b)
            f = count(lambda s, pos: (s == thr8) & (pos < cand))
            return jnp.where(f < need, cand, qpos)

        assert nt * TQ <= 2048
        qpos = lax.fori_loop(0, 11, pos_body, jnp.zeros((1, TQ), I32))
        qpos = jnp.where(excess, qpos, jnp.int32(2 ** 30))

        def body(c, carry):
            s = sc_ref[c]
            sel = (s > thr) | ((s == thr) & (c * TQ + key <= qpos))
            madd_ref[c] = jnp.where(sel, 0.0, -jnp.inf)
            return carry
        lax.fori_loop(0, i + 1, body, 0)

    _attend(i, nt, qT_ref, k_ref, vT_ref, bias_ref, lambda c, g, diag: ("tile", madd_ref[c]),
            s_ref, m_ref, acc_ref)
    _gated_out(acc_ref, sg_ref, x_ref, wout_ref, out_ref)


def _dsa_attention(x, qT, k2, vT, sg, iqT, iwT, ik, w_out, bias):
    B, T, _ = x.shape
    nt = T // TQ
    topk = min(DSA_TOPK_MAX, T // 4)
    sp = _attn_specs(T)
    return pl.pallas_call(
        functools.partial(_dsa_kernel, nt, topk),
        grid=(B, nt),
        in_specs=[sp["qT"], sp["row"](ATTN_WIDTH), sp["row"](D_MODEL),
                  pl.BlockSpec((1, IDX_HEADS, LANES, TQ), lambda b, i: (b, 0, 0, i)),
                  pl.BlockSpec((1, IDX_HEADS, TQ), lambda b, i: (b, 0, i)),
                  sp["k2"], sp["vT"],
                  pl.BlockSpec((1, T, LANES), lambda b, i: (b, 0, 0)),
                  sp["wout"], sp["bias"]],
        out_specs=sp["row"](D_MODEL),
        out_shape=jax.ShapeDtypeStruct((B, T, D_MODEL), F32),
        scratch_shapes=[pltpu.VMEM((nt, TQ, TQ), BF16), pltpu.VMEM((nt, TQ, TQ), F32)] + _attn_scratch(nt),
        compiler_params=_cparams(2), name="dsa_attention",
    )(qT, sg, x, iqT, iwT, k2, vT, ik, w_out.astype(BF16), bias)


def _split_bf16(a):
    hi = a.astype(BF16)
    return hi, (a - hi.astype(F32)).astype(BF16)


def _moba_kernel(nt, nsel, qT_ref, sg_ref, x_ref, k_ref, vT_ref, km_ref, wout_ref, bias_ref,
                 out_ref, sel_ref, s_ref, m_ref, acc_ref):
    i = pl.program_id(1)
    assert nt <= SUBLANES
    blk = lax.broadcasted_iota(I32, (SUBLANES, TQ), 0)

    km = jnp.concatenate([km_ref[0], jnp.zeros((2 * SUBLANES - nt, KV_WIDTH), F32)], axis=0)
    for g in range(N_KV_HEADS):
        qs = qT_ref[0, GROUP * g].astype(F32)
        for j in range(1, GROUP):
            qs = qs + qT_ref[0, GROUP * g + j].astype(F32)
        q_hi, q_lo = _split_bf16(qs)
        m_hi, m_lo = _split_bf16(km[:, (g // 2) * LANES:(g // 2 + 1) * LANES])
        gs = (_dot(m_hi, q_hi) + (_dot(m_hi, q_lo) + _dot(m_lo, q_hi)))[:SUBLANES]
        rank = jnp.zeros((SUBLANES, TQ), F32)
        for n in range(nt):
            gn = gs[n:n + 1, :]
            beats = (gn > gs) | ((gn == gs) & (n < blk))
            rank = rank + jnp.where(beats, jnp.where(n < i, 1.0, 0.0), 0.0)
        sel_ref[g] = jnp.where(((rank < nsel) & (blk < i)) | (blk == i), 1.0, 0.0)

    def mask_fn(c, g, diag):
        if diag:
            return None
        return ("col", sel_ref[g, pl.ds(c, 1), :] > 0.5)

    _attend(i, nt, qT_ref, k_ref, vT_ref, bias_ref, mask_fn, s_ref, m_ref, acc_ref)
    _gated_out(acc_ref, sg_ref, x_ref, wout_ref, out_ref)


def _moba_attention(x, qT, k2, vT, sg, kmean, w_out, bias):
    B, T, _ = x.shape
    nt = T // TQ
    nsel = min(MOBA_TOPK, nt - 1)
    sp = _attn_specs(T)
    return pl.pallas_call(
        functools.partial(_moba_kernel, nt, nsel),
        grid=(B, nt),
        in_specs=[sp["qT"], sp["row"](ATTN_WIDTH), sp["row"](D_MODEL), sp["k2"], sp["vT"],
                  pl.BlockSpec((1, nt, KV_WIDTH), lambda b, i: (b, 0, 0)),
                  sp["wout"], sp["bias"]],
        out_specs=sp["row"](D_MODEL),
        out_shape=jax.ShapeDtypeStruct((B, T, D_MODEL), F32),
        scratch_shapes=[pltpu.VMEM((N_KV_HEADS, SUBLANES, TQ), F32)] + _attn_scratch(nt),
        compiler_params=_cparams(2), name="moba_attention",
    )(qT, sg, x, k2, vT, kmean, w_out.astype(BF16), bias)


def kernel(x, norm_a_g, w_in_a, qn_a_g, kn_a_g, w_out_a, rel_bias, norm_kv_g, w_kv, kn_b_g,
           norm_b_g, w_in_b, qn_b_g, w_out_b):
    B, T, _ = x.shape
    assert norm_a_g.shape[0] == 1 and norm_b_g.shape[0] == 1
    bias = _bias_tiles(rel_bias)
    qT, k2, vT, sg, iqT, iwT, ik = _proj_a(x, norm_a_g[0], w_in_a[0], qn_a_g[0], kn_a_g[0])
    h = _dsa_attention(x, qT, k2, vT, sg, iqT, iwT, ik, w_out_a[0], bias)
    qT, k2, vT, sg, kmean = _proj_b(h, norm_kv_g, w_kv, kn_b_g, norm_b_g[0], w_in_b[0], qn_b_g[0])
    return _moba_attention(h, qT, k2, vT, sg, kmean.reshape(B, T // TQ, KV_WIDTH), w_out_b[0], bias)
```

```python
import functools
import math

import jax
import jax.numpy as jnp
import numpy as np
from jax import lax
from jax.experimental import pallas as pl
from jax.experimental.pallas import tpu as pltpu

D_MODEL = 1024
N_HEADS = 16
HEAD_DIM = 64
N_KV_HEADS = 4
GROUP = 4
ATTN_WIDTH = 1024
KV_WIDTH = 256
IDX_HEADS = 8
IDX_DIM = 64
DSA_TOPK_MAX = 256
MOBA_BLOCK = 256
MOBA_TOPK = 3
REL_BUCKETS = 32
REL_MAX_DIST = 128
EPS = 1e-6

LANES = 128
SUBLANES = 8
HALF = LANES // 2
TQ = 256
TM_A = 512
N_PAIRS = N_HEADS // 2
HB = 8
VT_ROWS = LANES + 2 * SUBLANES
LOG2E = 1.4426950408889634
Q_SCALE = HEAD_DIM ** -0.5 * LOG2E
NEG_BIG = -1e30
VMEM_LIMIT = 60 * 1024 * 1024

F32 = jnp.float32
BF16 = jnp.bfloat16
I32 = jnp.int32


def _bucket_starts():
    max_exact = REL_BUCKETS // 2
    n = np.arange(0, 4096)
    nf = np.maximum(n, 1).astype(np.float64)
    large = max_exact + (np.log(nf / max_exact) / math.log(REL_MAX_DIST / max_exact)
                         * (REL_BUCKETS - max_exact)).astype(np.int64)
    bucket = np.where(n < max_exact, n, np.minimum(large, REL_BUCKETS - 1))
    nf32 = np.maximum(n, 1).astype(np.float32)
    large32 = max_exact + (np.log(nf32 / np.float32(max_exact)) / np.float32(math.log(REL_MAX_DIST / max_exact))
                           * np.float32(REL_BUCKETS - max_exact)).astype(np.int32)
    bucket32 = np.where(n < max_exact, n, np.minimum(large32, REL_BUCKETS - 1))
    assert (bucket == bucket32).all() and (np.diff(bucket) >= 0).all()
    return [int(np.argmax(bucket >= b)) for b in range(REL_BUCKETS)]


BUCKET_STARTS = _bucket_starts()


def _cparams(n_axes):
    return pltpu.CompilerParams(dimension_semantics=("arbitrary",) * n_axes,
                                vmem_limit_bytes=VMEM_LIMIT)


def _dot_t(a, b):
    return lax.dot_general(a, b, (((1,), (1,)), ((), ())), preferred_element_type=F32)


def _dot(a, b):
    return jnp.dot(a, b, preferred_element_type=F32)


def _const_spec(shape):
    nd = len(shape)
    return pl.BlockSpec(shape, lambda *_: (0,) * nd, pipeline_mode=pl.Buffered(1))


def _bias_kernel(rb_ref, out_ref):
    kind = pl.program_id(0)
    h = pl.program_id(1)
    key = lax.broadcasted_iota(I32, (TQ, TQ), 0)
    qry = lax.broadcasted_iota(I32, (TQ, TQ), 1)
    n = qry - key + (1 - kind) * TQ
    last = rb_ref[REL_BUCKETS - 1, h]
    tile = jnp.full((TQ, TQ), last, F32)
    for b in range(REL_BUCKETS - 2, -1, -1):
        tile = jnp.where(n < BUCKET_STARTS[b + 1], rb_ref[b, h], tile)
    tile = (tile - last) * LOG2E
    out_ref[0, 0] = jnp.where(n < 0, -jnp.inf, tile)


def _bias_tiles(rel_bias):
    return pl.pallas_call(
        _bias_kernel,
        grid=(2, N_HEADS),
        in_specs=[pl.BlockSpec(memory_space=pltpu.SMEM)],
        out_specs=pl.BlockSpec((1, 1, TQ, TQ), lambda k, h: (k, h, 0, 0)),
        out_shape=jax.ShapeDtypeStruct((2, N_HEADS, TQ, TQ), F32),
        compiler_params=_cparams(2), name="bias_tiles",
    )(rel_bias)


def _rms_rows(x):
    return lax.rsqrt(jnp.mean(x * x, axis=-1, keepdims=True) + EPS)


def _half_norm(t, gain2):
    lo = lax.broadcasted_iota(I32, t.shape, 1) < HALF
    s = t * t
    s_lo = jnp.sum(jnp.where(lo, s, 0.0), axis=1, keepdims=True)
    s_hi = jnp.sum(jnp.where(lo, 0.0, s), axis=1, keepdims=True)
    r_lo = lax.rsqrt(s_lo * (1.0 / HEAD_DIM) + EPS)
    r_hi = lax.rsqrt(s_hi * (1.0 / HEAD_DIM) + EPS)
    return t * jnp.where(lo, r_lo, r_hi) * gain2


def _in_half(slab, half):
    z = jnp.zeros_like(slab)
    return jnp.concatenate([slab, z] if half == 0 else [z, slab], axis=0)


def _store_qT(qT_ref, yqT, gain_col):
    for h in range(N_HEADS):
        slab = yqT[h * HEAD_DIM:(h + 1) * HEAD_DIM, :]
        r = lax.rsqrt(jnp.sum(slab * slab, axis=0, keepdims=True) * (1.0 / HEAD_DIM) + EPS)
        qn = slab * r * gain_col * Q_SCALE
        qT_ref[0, h] = _in_half(qn, (h // GROUP) % 2).astype(BF16)


def _store_vT(vT_ref, yvT):
    m = yvT.shape[1]
    ones = jnp.ones((SUBLANES, m), F32)
    zeros = jnp.zeros((SUBLANES, m), F32)
    for g in range(N_KV_HEADS):
        slab = yvT[g * HEAD_DIM:(g + 1) * HEAD_DIM, :]
        for e in range(2):
            tail = [ones, zeros] if e == 0 else [zeros, ones]
            tile = jnp.concatenate([_in_half(slab, e)] + tail, axis=0).astype(BF16)
            for cc in range(m // TQ):
                vT_ref[0, 2 * g + e, cc] = tile[:, cc * TQ:(cc + 1) * TQ]


def _silu(y):
    return y * jax.nn.sigmoid(y)


def _proj_out_specs(tm, T):
    nt = T // TQ
    specs = dict(
        qT=(pl.BlockSpec((1, N_HEADS, LANES, tm), lambda b, i: (b, 0, 0, i)),
            lambda B: jax.ShapeDtypeStruct((B, N_HEADS, LANES, T), BF16)),
        k2=(pl.BlockSpec((1, 2, tm, LANES), lambda b, i: (b, 0, i, 0)),
            lambda B: jax.ShapeDtypeStruct((B, 2, T, LANES), BF16)),
        vT=(pl.BlockSpec((1, 2 * N_KV_HEADS, tm // TQ, VT_ROWS, TQ), lambda b, i: (b, 0, i, 0, 0)),
            lambda B: jax.ShapeDtypeStruct((B, 2 * N_KV_HEADS, nt, VT_ROWS, TQ), BF16)),
        sg=(pl.BlockSpec((1, tm, ATTN_WIDTH), lambda b, i: (b, i, 0)),
            lambda B: jax.ShapeDtypeStruct((B, T, ATTN_WIDTH), BF16)),
    )
    return specs


def _proj_a_kernel(x_ref, g_ref, wq_ref, wn_ref, wr_ref, qg_ref, kg_ref,
                   qT_ref, k_ref, vT_ref, sg_ref, iqT_ref, iwT_ref, ik_ref):
    x = x_ref[0]
    h = (x * _rms_rows(x) * g_ref[...]).astype(BF16)
    _store_qT(qT_ref, _dot_t(wq_ref[...], h), qg_ref[...])
    yn = _dot(h, wn_ref[...])
    for m in range(2):
        k_ref[0, m] = _half_norm(yn[:, m * LANES:(m + 1) * LANES], kg_ref[...]).astype(BF16)
    sg_ref[0] = _silu(yn[:, KV_WIDTH:KV_WIDTH + ATTN_WIDTH]).astype(BF16)
    ik_ref[0] = yn[:, KV_WIDTH + ATTN_WIDTH:].astype(BF16)
    yr = _dot_t(wr_ref[...], h)
    _store_vT(vT_ref, yr[:KV_WIDTH])
    for hh in range(IDX_HEADS):
        slab = yr[KV_WIDTH + hh * IDX_DIM:KV_WIDTH + (hh + 1) * IDX_DIM, :]
        iqT_ref[0, hh] = _in_half(slab, 0).astype(BF16)
    iw0 = KV_WIDTH + IDX_HEADS * IDX_DIM
    iwT_ref[0] = yr[iw0:iw0 + IDX_HEADS, :] * (IDX_HEADS ** -0.5 * IDX_DIM ** -0.5)


def _proj_a(x, norm_g, w_in, qn_g, kn_g):
    B, T, _ = x.shape
    c = np.cumsum([0, ATTN_WIDTH, KV_WIDTH, KV_WIDTH, ATTN_WIDTH, IDX_HEADS * IDX_DIM, IDX_HEADS, IDX_DIM])
    part = lambda j: w_in[:, c[j]:c[j + 1]]
    wq_t = part(0).T.astype(BF16)
    wn = jnp.concatenate([part(1), part(3), jnp.pad(part(6), ((0, 0), (0, LANES - IDX_DIM)))],
                         axis=1).astype(BF16)
    wr_t = jnp.concatenate([part(2), part(4), jnp.pad(part(5), ((0, 0), (0, IDX_HEADS)))],
                           axis=1).T.astype(BF16)
    so = _proj_out_specs(TM_A, T)
    outs = [so["qT"], so["k2"], so["vT"], so["sg"],
            (pl.BlockSpec((1, IDX_HEADS, LANES, TM_A), lambda b, i: (b, 0, 0, i)),
             lambda B: jax.ShapeDtypeStruct((B, IDX_HEADS, LANES, T), BF16)),
            (pl.BlockSpec((1, IDX_HEADS, TM_A), lambda b, i: (b, 0, i)),
             lambda B: jax.ShapeDtypeStruct((B, IDX_HEADS, T), F32)),
            (pl.BlockSpec((1, TM_A, LANES), lambda b, i: (b, i, 0)),
             lambda B: jax.ShapeDtypeStruct((B, T, LANES), BF16))]
    return pl.pallas_call(
        _proj_a_kernel,
        grid=(B, T // TM_A),
        in_specs=[pl.BlockSpec((1, TM_A, D_MODEL), lambda b, i: (b, i, 0)),
                  _const_spec((1, D_MODEL)), _const_spec(wq_t.shape), _const_spec(wn.shape),
                  _const_spec(wr_t.shape), _const_spec((HEAD_DIM, 1)), _const_spec((1, LANES))],
        out_specs=[o[0] for o in outs],
        out_shape=[o[1](B) for o in outs],
        compiler_params=_cparams(2), name="proj_a",
    )(x, norm_g.reshape(1, D_MODEL), wq_t, wn, wr_t, qn_g.reshape(HEAD_DIM, 1),
      jnp.concatenate([kn_g, kn_g]).reshape(1, LANES))


def _proj_b_kernel(x_ref, gkv_ref, gb_ref, wk_ref, wv_ref, wq_ref, wg_ref, kg_ref, qg_ref,
                   qT_ref, k_ref, vT_ref, sg_ref, km_ref):
    x = x_ref[0]
    xr = x * _rms_rows(x)
    hkv = (xr * gkv_ref[...]).astype(BF16)
    hb = (xr * gb_ref[...]).astype(BF16)
    yk = _dot(hkv, wk_ref[...])
    means = []
    for m in range(2):
        kn = _half_norm(yk[:, m * LANES:(m + 1) * LANES], kg_ref[...])
        k_ref[0, m] = kn.astype(BF16)
        means.append(jnp.sum(kn, axis=0, keepdims=True) * (1.0 / MOBA_BLOCK))
    km_ref[0, 0] = jnp.concatenate(means, axis=1)
    _store_vT(vT_ref, _dot_t(wv_ref[...], hkv))
    _store_qT(qT_ref, _dot_t(wq_ref[...], hb), qg_ref[...])
    sg_ref[0] = _silu(_dot(hb, wg_ref[...])).astype(BF16)


def _proj_b(x, norm_kv_g, w_kv, kn_g, norm_b_g, w_in_b, qn_g):
    B, T, _ = x.shape
    assert T % MOBA_BLOCK == 0 and TQ == MOBA_BLOCK
    wk = w_kv[:, :KV_WIDTH].astype(BF16)
    wv_t = w_kv[:, KV_WIDTH:].T.astype(BF16)
    wq_t = w_in_b[:, :ATTN_WIDTH].T.astype(BF16)
    wg = w_in_b[:, ATTN_WIDTH:].astype(BF16)
    so = _proj_out_specs(TQ, T)
    outs = [so["qT"], so["k2"], so["vT"], so["sg"],
            (pl.BlockSpec((1, 1, 1, KV_WIDTH), lambda b, i: (b, i, 0, 0)),
             lambda B: jax.ShapeDtypeStruct((B, T // TQ, 1, KV_WIDTH), F32))]
    return pl.pallas_call(
        _proj_b_kernel,
        grid=(B, T // TQ),
        in_specs=[pl.BlockSpec((1, TQ, D_MODEL), lambda b, i: (b, i, 0)),
                  _const_spec((1, D_MODEL)), _const_spec((1, D_MODEL)),
                  _const_spec(wk.shape), _const_spec(wv_t.shape), _const_spec(wq_t.shape),
                  _const_spec(wg.shape), _const_spec((1, LANES)), _const_spec((HEAD_DIM, 1))],
        out_specs=[o[0] for o in outs],
        out_shape=[o[1](B) for o in outs],
        compiler_params=_cparams(2), name="proj_b",
    )(x, norm_kv_g.reshape(1, D_MODEL), norm_b_g.reshape(1, D_MODEL), wk, wv_t, wq_t, wg,
      jnp.concatenate([kn_g, kn_g]).reshape(1, LANES), qn_g.reshape(HEAD_DIM, 1))


def _two_per_body(n, fn):
    def body(j, carry):
        fn(2 * j, 2 * j + 1)
        return carry

    lax.fori_loop(0, lax.shift_right_logical(n, 1), body, 0)

    @pl.when((n & 1) == 1)
    def _():
        fn(n - 1)


def _attend(i, nt, qT_ref, k_ref, vT_ref, bias_ref, mask_fn, s_ref, m_ref, acc_ref):
    n_groups = N_HEADS // HB
    kv_per_group = HB // GROUP
    m_ref[...] = jnp.full(m_ref.shape, NEG_BIG, F32)
    acc_ref[...] = jnp.zeros(acc_ref.shape, F32)

    def slot(o, c):
        return c if o % 2 == 0 else jnp.where(c >= 2, c - 2, nt + c)

    def scores(o, c, bias_kind):
        koff = pl.multiple_of(c * TQ, TQ)
        base = slot(o, c) * HB
        for gi in range(kv_per_group):
            g = o * kv_per_group + gi
            kt = k_ref[0, g // 2, pl.ds(koff, TQ), :]
            mask = mask_fn(c, g, bias_kind == 1)
            for hl in range(GROUP * gi, GROUP * (gi + 1)):
                hd = o * HB + hl
                s = _dot(kt, qT_ref[0, hd])
                if bias_kind is not None:
                    s = s + bias_ref[bias_kind, hd]
                if mask is not None and mask[0] == "tile":
                    s = s + mask[1]
                s_ref[base + hl] = s
                m_cur = jnp.max(s, axis=0, keepdims=True)
                if mask is not None and mask[0] == "col":
                    m_cur = jnp.where(mask[1], m_cur, NEG_BIG)
                m_ref[hd] = jnp.maximum(m_ref[hd], m_cur)

    def values(o, c):
        base = slot(o, c) * HB
        for gi in range(kv_per_group):
            g = o * kv_per_group + gi
            mask = mask_fn(c, g, False)
            for pj in range(2):
                pair = 2 * g + pj
                pv = None
                for e in range(2):
                    hd = 2 * pair + e
                    m = m_ref[hd]
                    if mask is not None and mask[0] == "col":
                        m = jnp.where(mask[1], m, jnp.inf)
                    pr = jnp.exp2((s_ref[base + hd - o * HB] - m).astype(BF16))
                    d = _dot(vT_ref[0, 2 * g + e, c], pr)
                    pv = d if pv is None else pv + d
                acc_ref[pair] = acc_ref[pair] + pv

    far = jnp.maximum(i - 1, 0)
    for o in range(n_groups + 1):
        def far_chunks(*cs, o=o):
            for c in cs:
                if o >= 1:
                    values(o - 1, c)
            for c in cs:
                if o < n_groups:
                    scores(o, c, None)

        _two_per_body(far, far_chunks)

        @pl.when(i >= 1)
        def _(o=o):
            if o >= 1:
                values(o - 1, i - 1)
                values(o - 1, i)
            if o < n_groups:
                scores(o, i - 1, 0)
                scores(o, i, 1)

        @pl.when(i == 0)
        def _(o=o):
            if o >= 1:
                values(o - 1, i)
            if o < n_groups:
                scores(o, i, 1)


def _gated_out(acc_ref, sg_ref, x_ref, wout_ref, out_ref):
    og = []
    for p in range(N_PAIRS):
        acc = acc_ref[p]
        inv0 = 1.0 / acc[LANES:LANES + 1, :]
        inv1 = 1.0 / acc[LANES + SUBLANES:LANES + SUBLANES + 1, :]
        o_t = jnp.concatenate([acc[0:HALF, :] * inv0, acc[HALF:LANES, :] * inv1], axis=0)
        og.append((o_t.T * sg_ref[0, :, p * LANES:(p + 1) * LANES].astype(F32)).astype(BF16))
    out_ref[0] = x_ref[0] + _dot(jnp.concatenate(og, axis=1), wout_ref[...])


def _attn_scratch(nt):
    return [pltpu.VMEM(((nt + 2) * HB, TQ, TQ), F32),
            pltpu.VMEM((N_HEADS, 1, TQ), F32),
            pltpu.VMEM((N_PAIRS, VT_ROWS, TQ), F32)]


def _attn_specs(T):
    nt = T // TQ
    return dict(
        qT=pl.BlockSpec((1, N_HEADS, LANES, TQ), lambda b, i: (b, 0, 0, i)),
        k2=pl.BlockSpec((1, 2, T, LANES), lambda b, i: (b, 0, 0, 0)),
        vT=pl.BlockSpec((1, 2 * N_KV_HEADS, nt, VT_ROWS, TQ), lambda b, i: (b, 0, 0, 0, 0)),
        row=lambda n: pl.BlockSpec((1, TQ, n), lambda b, i: (b, i, 0)),
        wout=_const_spec((ATTN_WIDTH, D_MODEL)),
        bias=_const_spec((2, N_HEADS, TQ, TQ)),
    )


INT_MIN = -2 ** 31
KEY_NEG_INF = -0x7F800000


def _key_to_float(k):
    bits = jnp.where(k < 0, jnp.int32(INT_MIN) - k, k)
    return lax.bitcast_convert_type(bits, F32)


def _dsa_kernel(nt, topk, qT_ref, sg_ref, x_ref, iqT_ref, iwT_ref, k_ref, vT_ref, ik_ref,
                wout_ref, bias_ref, out_ref, sc16_ref, madd_ref, s_ref, m_ref, acc_ref):
    i = pl.program_id(1)
    key = lax.broadcasted_iota(I32, (TQ, TQ), 0)
    qry = lax.broadcasted_iota(I32, (TQ, TQ), 1)
    sub = lax.broadcasted_iota(I32, (SUBLANES, TQ), 0)
    assert nt <= 2 * HB
    sc_ref = s_ref.at[pl.ds(nt * HB, nt)]

    iw = iwT_ref[0]

    def score_chunk(c):
        ik = ik_ref[0, pl.ds(pl.multiple_of(c * TQ, TQ), TQ), :]
        acc = jnp.zeros((TQ, TQ), F32)
        for h in range(IDX_HEADS):
            acc = acc + iw[h:h + 1, :] * jnp.maximum(_dot(ik, iqT_ref[0, h]), 0.0)
        return acc

    def store_scores(c, s):
        sc_ref[c] = s
        sc16_ref[c] = s.astype(BF16)

    _two_per_body(i, lambda *cs: [store_scores(c, score_chunk(c)) for c in cs])
    store_scores(i, jnp.where(key > qry, -jnp.inf, score_chunk(i)))

    def count(pred):
        def body(c, tots):
            tots = list(tots)
            for r in range(TQ // SUBLANES):
                s = sc_ref[c, r * SUBLANES:(r + 1) * SUBLANES, :]
                hit = pred(s, c * TQ + r * SUBLANES + sub)
                tots[r % len(tots)] = tots[r % len(tots)] + jnp.where(hit, 1.0, 0.0)
            return tuple(tots)
        zero = jnp.zeros((SUBLANES, TQ), F32)
        t = lax.fori_loop(0, i + 1, body, (zero,) * 4)
        return jnp.sum((t[0] + t[1]) + (t[2] + t[3]), axis=0, keepdims=True)

    def count16(cand16):
        rows = 2 * SUBLANES
        one, zero = jnp.ones((rows, TQ), BF16), jnp.zeros((rows, TQ), BF16)

        def body(c, tots):
            tots = list(tots)
            for r in range(TQ // rows):
                s = sc16_ref[c, r * rows:(r + 1) * rows, :]
                tots[r % len(tots)] = tots[r % len(tots)] + jnp.where(s >= cand16, one, zero)
            return tuple(tots)
        assert nt * (TQ // rows) <= 256
        t = lax.fori_loop(0, i + 1, body, (zero,) * 4)
        return jnp.sum(((t[0] + t[1]) + (t[2] + t[3])).astype(F32), axis=0, keepdims=True)

    def coarse_body(b, u):
        uc = u | lax.shift_left(jnp.int32(1), 31 - b)
        in_range = (uc >= 0) | (uc <= -(1 << 24))
        cand = _key_to_float(jnp.int32(KEY_NEG_INF) + uc)
        cnt = count16(jnp.broadcast_to(cand.astype(BF16), (2 * SUBLANES, TQ)))
        return jnp.where(in_range & (cnt >= topk), uc, u)

    k16 = jnp.int32(KEY_NEG_INF) + lax.fori_loop(0, 16, coarse_body, jnp.zeros((1, TQ), I32))

    base = jnp.maximum(k16 - (1 << 16), jnp.int32(KEY_NEG_INF))

    def fine_body(b, carry):
        u, cnt_t = carry
        uc = u | lax.shift_left(jnp.int32(1), 16 - b)
        cand = jnp.broadcast_to(_key_to_float(base + uc), (SUBLANES, TQ))
        cnt = count(lambda s, pos: s >= cand)
        ok = cnt >= topk
        return jnp.where(ok, uc, u), jnp.where(ok, cnt, cnt_t)

    cnt0 = jnp.full((1, TQ), float(nt * TQ), F32)
    u, cnt_t = lax.fori_loop(0, 17, fine_body, (jnp.zeros((1, TQ), I32), cnt0))
    thr = _key_to_float(base + u)

    excess = (cnt_t > topk) & (thr > -jnp.inf)
    any_excess = jnp.max(jnp.where(excess, 1.0, 0.0)) > 0.0

    @pl.when(jnp.logical_not(any_excess))
    def _():
        def body(c, carry):
            madd_ref[c] = jnp.where(sc_ref[c] >= thr, 0.0, -jnp.inf)
            return carry
        lax.fori_loop(0, i + 1, body, 0)

    @pl.when(any_excess)
    def _():
        thr8 = jnp.broadcast_to(thr, (SUBLANES, TQ))
        need = topk - count(lambda s, pos: s > thr8)

        def pos_body(b, qpos):
            cand = qpos | lax.shift_left(jnp.int32(1), 10 - b)
            f = count(lambda s, pos: (s == thr8) & (pos < cand))
            return jnp.where(f < need, cand, qpos)

        assert nt * TQ <= 2048
        qpos = lax.fori_loop(0, 11, pos_body, jnp.zeros((1, TQ), I32))
        qpos = jnp.where(excess, qpos, jnp.int32(2 ** 30))

        def body(c, carry):
            s = sc_ref[c]
            sel = (s > thr) | ((s == thr) & (c * TQ + key <= qpos))
            madd_ref[c] = jnp.where(sel, 0.0, -jnp.inf)
            return carry
        lax.fori_loop(0, i + 1, body, 0)

    _attend(i, nt, qT_ref, k_ref, vT_ref, bias_ref, lambda c, g, diag: ("tile", madd_ref[c]),
            s_ref, m_ref, acc_ref)
    _gated_out(acc_ref, sg_ref, x_ref, wout_ref, out_ref)


def _dsa_attention(x, qT, k2, vT, sg, iqT, iwT, ik, w_out, bias):
    B, T, _ = x.shape
    nt = T // TQ
    topk = min(DSA_TOPK_MAX, T // 4)
    sp = _attn_specs(T)
    return pl.pallas_call(
        functools.partial(_dsa_kernel, nt, topk),
        grid=(B, nt),
        in_specs=[sp["qT"], sp["row"](ATTN_WIDTH), sp["row"](D_MODEL),
                  pl.BlockSpec((1, IDX_HEADS, LANES, TQ), lambda b, i: (b, 0, 0, i)),
                  pl.BlockSpec((1, IDX_HEADS, TQ), lambda b, i: (b, 0, i)),
                  sp["k2"], sp["vT"],
                  pl.BlockSpec((1, T, LANES), lambda b, i: (b, 0, 0)),
                  sp["wout"], sp["bias"]],
        out_specs=sp["row"](D_MODEL),
        out_shape=jax.ShapeDtypeStruct((B, T, D_MODEL), F32),
        scratch_shapes=[pltpu.VMEM((nt, TQ, TQ), BF16), pltpu.VMEM((nt, TQ, TQ), F32)] + _attn_scratch(nt),
        compiler_params=_cparams(2), name="dsa_attention",
    )(qT, sg, x, iqT, iwT, k2, vT, ik, w_out.astype(BF16), bias)


def _split_bf16(a):
    hi = a.astype(BF16)
    return hi, (a - hi.astype(F32)).astype(BF16)


def _moba_kernel(nt, nsel, qT_ref, sg_ref, x_ref, k_ref, vT_ref, km_ref, wout_ref, bias_ref,
                 out_ref, sel_ref, s_ref, m_ref, acc_ref):
    i = pl.program_id(1)
    assert nt <= SUBLANES
    blk = lax.broadcasted_iota(I32, (SUBLANES, TQ), 0)

    km = jnp.concatenate([km_ref[0], jnp.zeros((2 * SUBLANES - nt, KV_WIDTH), F32)], axis=0)
    for g in range(N_KV_HEADS):
        qs = qT_ref[0, GROUP * g].astype(F32)
        for j in range(1, GROUP):
            qs = qs + qT_ref[0, GROUP * g + j].astype(F32)
        q_hi, q_lo = _split_bf16(qs)
        m_hi, m_lo = _split_bf16(km[:, (g // 2) * LANES:(g // 2 + 1) * LANES])
        gs = (_dot(m_hi, q_hi) + (_dot(m_hi, q_lo) + _dot(m_lo, q_hi)))[:SUBLANES]
        rank = jnp.zeros((SUBLANES, TQ), F32)
        for n in range(nt):
            gn = gs[n:n + 1, :]
            beats = (gn > gs) | ((gn == gs) & (n < blk))
            rank = rank + jnp.where(beats, jnp.where(n < i, 1.0, 0.0), 0.0)
        sel_ref[g] = jnp.where(((rank < nsel) & (blk < i)) | (blk == i), 1.0, 0.0)

    def mask_fn(c, g, diag):
        if diag:
            return None
        return ("col", sel_ref[g, pl.ds(c, 1), :] > 0.5)

    _attend(i, nt, qT_ref, k_ref, vT_ref, bias_ref, mask_fn, s_ref, m_ref, acc_ref)
    _gated_out(acc_ref, sg_ref, x_ref, wout_ref, out_ref)


def _moba_attention(x, qT, k2, vT, sg, kmean, w_out, bias):
    B, T, _ = x.shape
    nt = T // TQ
    nsel = min(MOBA_TOPK, nt - 1)
    sp = _attn_specs(T)
    return pl.pallas_call(
        functools.partial(_moba_kernel, nt, nsel),
        grid=(B, nt),
        in_specs=[sp["qT"], sp["row"](ATTN_WIDTH), sp["row"](D_MODEL), sp["k2"], sp["vT"],
                  pl.BlockSpec((1, nt, KV_WIDTH), lambda b, i: (b, 0, 0)),
                  sp["wout"], sp["bias"]],
        out_specs=sp["row"](D_MODEL),
        out_shape=jax.ShapeDtypeStruct((B, T, D_MODEL), F32),
        scratch_shapes=[pltpu.VMEM((N_KV_HEADS, SUBLANES, TQ), F32)] + _attn_scratch(nt),
        compiler_params=_cparams(2), name="moba_attention",
    )(qT, sg, x, k2, vT, kmean, w_out.astype(BF16), bias)


def kernel(x, norm_a_g, w_in_a, qn_a_g, kn_a_g, w_out_a, rel_bias, norm_kv_g, w_kv, kn_b_g,
           norm_b_g, w_in_b, qn_b_g, w_out_b):
    B, T, _ = x.shape
    assert norm_a_g.shape[0] == 1 and norm_b_g.shape[0] == 1
    bias = _bias_tiles(rel_bias)
    qT, k2, vT, sg, iqT, iwT, ik = _proj_a(x, norm_a_g[0], w_in_a[0], qn_a_g[0], kn_a_g[0])
    h = _dsa_attention(x, qT, k2, vT, sg, iqT, iwT, ik, w_out_a[0], bias)
    qT, k2, vT, sg, kmean = _proj_b(h, norm_kv_g, w_kv, kn_b_g, norm_b_g[0], w_in_b[0], qn_b_g[0])
    return _moba_attention(h, qT, k2, vT, sg, kmean.reshape(B, T // TQ, KV_WIDTH), w_out_b[0], bias)
```

```python
import functools
import math

import jax
import jax.numpy as jnp
import numpy as np
from jax import lax
from jax.experimental import pallas as pl
from jax.experimental.pallas import tpu as pltpu

D_MODEL = 1024
N_HEADS = 16
HEAD_DIM = 64
N_KV_HEADS = 4
GROUP = 4
ATTN_WIDTH = 1024
KV_WIDTH = 256
IDX_HEADS = 8
IDX_DIM = 64
DSA_TOPK_MAX = 256
MOBA_BLOCK = 256
MOBA_TOPK = 3
REL_BUCKETS = 32
REL_MAX_DIST = 128
EPS = 1e-6

LANES = 128
SUBLANES = 8
HALF = LANES // 2
TQ = 256
TM_A = 512
N_PAIRS = N_HEADS // 2
HB = 8
VT_ROWS = LANES + 2 * SUBLANES
LOG2E = 1.4426950408889634
Q_SCALE = HEAD_DIM ** -0.5 * LOG2E
NEG_BIG = -1e30
V7X_VMEM_BYTES = 64 * 1024 * 1024

F32 = jnp.float32
BF16 = jnp.bfloat16
I32 = jnp.int32


def _bucket_starts():
    max_exact = REL_BUCKETS // 2
    n = np.arange(0, 2 * REL_MAX_DIST)
    nf = np.maximum(n, 1).astype(np.float64)
    large = max_exact + (np.log(nf / max_exact) / math.log(REL_MAX_DIST / max_exact)
                         * (REL_BUCKETS - max_exact)).astype(np.int64)
    bucket = np.where(n < max_exact, n, np.minimum(large, REL_BUCKETS - 1))
    nf32 = np.maximum(n, 1).astype(np.float32)
    large32 = max_exact + (np.log(nf32 / np.float32(max_exact)) / np.float32(math.log(REL_MAX_DIST / max_exact))
                           * np.float32(REL_BUCKETS - max_exact)).astype(np.int32)
    bucket32 = np.where(n < max_exact, n, np.minimum(large32, REL_BUCKETS - 1))
    assert (bucket == bucket32).all() and (np.diff(bucket) >= 0).all()
    return [int(np.argmax(bucket >= b)) for b in range(REL_BUCKETS)]


BUCKET_STARTS = _bucket_starts()


def _nbytes(shape, dtype):
    return math.prod(shape) * jnp.dtype(dtype).itemsize


def _cparams(n_axes, specs_and_dtypes, scratch=(), live_bytes=0):
    vmem = live_bytes + sum(_nbytes(s.shape, s.dtype) for s in scratch)
    for spec, dtype in specs_and_dtypes:
        if spec.block_shape is not None:
            buffers = spec.pipeline_mode.buffer_count if spec.pipeline_mode else 2
            vmem += buffers * _nbytes(spec.block_shape, dtype)
    assert vmem <= V7X_VMEM_BYTES, vmem
    return pltpu.CompilerParams(dimension_semantics=("arbitrary",) * n_axes,
                                vmem_limit_bytes=vmem)


def _call(kernel_fn, name, grid, in_specs, operands, out_specs, out_shapes, scratch=(), live_bytes=0):
    windows = [(s, a.dtype) for s, a in zip(in_specs, operands)]
    windows += [(s, o.dtype) for s, o in zip(out_specs, out_shapes)]
    return pl.pallas_call(
        kernel_fn, grid=grid, in_specs=in_specs, out_specs=out_specs, out_shape=out_shapes,
        scratch_shapes=list(scratch), name=name,
        compiler_params=_cparams(len(grid), windows, scratch, live_bytes),
    )(*operands)


def _dot_t(a, b):
    return lax.dot_general(a, b, (((1,), (1,)), ((), ())), preferred_element_type=F32)


def _dot(a, b):
    return jnp.dot(a, b, preferred_element_type=F32)


def _const_spec(shape):
    nd = len(shape)
    return pl.BlockSpec(shape, lambda *_: (0,) * nd, pipeline_mode=pl.Buffered(1))


def _bias_kernel(rb_ref, out_ref):
    kind = pl.program_id(0)
    h = pl.program_id(1)
    key = lax.broadcasted_iota(I32, (TQ, TQ), 0)
    qry = lax.broadcasted_iota(I32, (TQ, TQ), 1)
    n = qry - key + (1 - kind) * TQ
    last = rb_ref[REL_BUCKETS - 1, h]
    tile = jnp.full((TQ, TQ), last, F32)
    for b in range(REL_BUCKETS - 2, -1, -1):
        tile = jnp.where(n < BUCKET_STARTS[b + 1], rb_ref[b, h], tile)
    tile = (tile - last) * LOG2E
    out_ref[0, 0] = jnp.where(n < 0, -jnp.inf, tile)


def _bias_tiles(rel_bias):
    return _call(_bias_kernel, "bias_tiles", (2, N_HEADS),
                 [pl.BlockSpec(memory_space=pltpu.SMEM)], [rel_bias],
                 [pl.BlockSpec((1, 1, TQ, TQ), lambda k, h: (k, h, 0, 0))],
                 [jax.ShapeDtypeStruct((2, N_HEADS, TQ, TQ), F32)],
                 live_bytes=4 * _nbytes((TQ, TQ), F32))[0]


def _rms_rows(x):
    return lax.rsqrt(jnp.mean(x * x, axis=-1, keepdims=True) + EPS)


def _half_norm(t, gain2):
    lo = lax.broadcasted_iota(I32, t.shape, 1) < HALF
    s = t * t
    s_lo = jnp.sum(jnp.where(lo, s, 0.0), axis=1, keepdims=True)
    s_hi = jnp.sum(jnp.where(lo, 0.0, s), axis=1, keepdims=True)
    r_lo = lax.rsqrt(s_lo * (1.0 / HEAD_DIM) + EPS)
    r_hi = lax.rsqrt(s_hi * (1.0 / HEAD_DIM) + EPS)
    return t * jnp.where(lo, r_lo, r_hi) * gain2


def _in_half(slab, half):
    z = jnp.zeros_like(slab)
    return jnp.concatenate([slab, z] if half == 0 else [z, slab], axis=0)


def _store_qT(qT_ref, yqT, gain_col):
    for h in range(N_HEADS):
        slab = yqT[h * HEAD_DIM:(h + 1) * HEAD_DIM, :]
        r = lax.rsqrt(jnp.sum(slab * slab, axis=0, keepdims=True) * (1.0 / HEAD_DIM) + EPS)
        qn = slab * r * gain_col * Q_SCALE
        qT_ref[0, h] = _in_half(qn, (h // GROUP) % 2).astype(BF16)


def _store_vT(vT_ref, yvT):
    m = yvT.shape[1]
    ones = jnp.ones((SUBLANES, m), F32)
    zeros = jnp.zeros((SUBLANES, m), F32)
    for g in range(N_KV_HEADS):
        slab = yvT[g * HEAD_DIM:(g + 1) * HEAD_DIM, :]
        for e in range(2):
            tail = [ones, zeros] if e == 0 else [zeros, ones]
            tile = jnp.concatenate([_in_half(slab, e)] + tail, axis=0).astype(BF16)
            for cc in range(m // TQ):
                vT_ref[0, 2 * g + e, cc] = tile[:, cc * TQ:(cc + 1) * TQ]


def _silu(y):
    return y * jax.nn.sigmoid(y)


def _proj_out_specs(tm, T):
    nt = T // TQ
    specs = dict(
        qT=(pl.BlockSpec((1, N_HEADS, LANES, tm), lambda b, i: (b, 0, 0, i)),
            lambda B: jax.ShapeDtypeStruct((B, N_HEADS, LANES, T), BF16)),
        k2=(pl.BlockSpec((1, 2, tm, LANES), lambda b, i: (b, 0, i, 0)),
            lambda B: jax.ShapeDtypeStruct((B, 2, T, LANES), BF16)),
        vT=(pl.BlockSpec((1, 2 * N_KV_HEADS, tm // TQ, VT_ROWS, TQ), lambda b, i: (b, 0, i, 0, 0)),
            lambda B: jax.ShapeDtypeStruct((B, 2 * N_KV_HEADS, nt, VT_ROWS, TQ), BF16)),
        sg=(pl.BlockSpec((1, tm, ATTN_WIDTH), lambda b, i: (b, i, 0)),
            lambda B: jax.ShapeDtypeStruct((B, T, ATTN_WIDTH), BF16)),
    )
    return specs


def _proj_a_kernel(x_ref, g_ref, wq_ref, wn_ref, wr_ref, qg_ref, kg_ref,
                   qT_ref, k_ref, vT_ref, sg_ref, iqT_ref, iwT_ref, ik_ref):
    x = x_ref[0]
    h = (x * _rms_rows(x) * g_ref[...]).astype(BF16)
    _store_qT(qT_ref, _dot_t(wq_ref[...], h), qg_ref[...])
    yn = _dot(h, wn_ref[...])
    for m in range(2):
        k_ref[0, m] = _half_norm(yn[:, m * LANES:(m + 1) * LANES], kg_ref[...]).astype(BF16)
    sg_ref[0] = _silu(yn[:, KV_WIDTH:KV_WIDTH + ATTN_WIDTH]).astype(BF16)
    ik_ref[0] = yn[:, KV_WIDTH + ATTN_WIDTH:].astype(BF16)
    yr = _dot_t(wr_ref[...], h)
    _store_vT(vT_ref, yr[:KV_WIDTH])
    for hh in range(IDX_HEADS):
        slab = yr[KV_WIDTH + hh * IDX_DIM:KV_WIDTH + (hh + 1) * IDX_DIM, :]
        iqT_ref[0, hh] = _in_half(slab, 0).astype(BF16)
    iw0 = KV_WIDTH + IDX_HEADS * IDX_DIM
    iwT_ref[0] = yr[iw0:iw0 + IDX_HEADS, :] * (IDX_HEADS ** -0.5 * IDX_DIM ** -0.5)


def _proj_a(x, norm_g, w_in, qn_g, kn_g):
    B, T, _ = x.shape
    c = np.cumsum([0, ATTN_WIDTH, KV_WIDTH, KV_WIDTH, ATTN_WIDTH, IDX_HEADS * IDX_DIM, IDX_HEADS, IDX_DIM])
    part = lambda j: w_in[:, c[j]:c[j + 1]]
    wq_t = part(0).T.astype(BF16)
    wn = jnp.concatenate([part(1), part(3), jnp.pad(part(6), ((0, 0), (0, LANES - IDX_DIM)))],
                         axis=1).astype(BF16)
    wr_t = jnp.concatenate([part(2), part(4), jnp.pad(part(5), ((0, 0), (0, IDX_HEADS)))],
                           axis=1).T.astype(BF16)
    so = _proj_out_specs(TM_A, T)
    outs = [so["qT"], so["k2"], so["vT"], so["sg"],
            (pl.BlockSpec((1, IDX_HEADS, LANES, TM_A), lambda b, i: (b, 0, 0, i)),
             lambda B: jax.ShapeDtypeStruct((B, IDX_HEADS, LANES, T), BF16)),
            (pl.BlockSpec((1, IDX_HEADS, TM_A), lambda b, i: (b, 0, i)),
             lambda B: jax.ShapeDtypeStruct((B, IDX_HEADS, T), F32)),
            (pl.BlockSpec((1, TM_A, LANES), lambda b, i: (b, i, 0)),
             lambda B: jax.ShapeDtypeStruct((B, T, LANES), BF16))]
    live = (_nbytes((TM_A, D_MODEL), F32) + _nbytes((TM_A, D_MODEL), BF16)
            + _nbytes((TM_A, wq_t.shape[0] + wn.shape[1] + wr_t.shape[0]), F32))
    return _call(
        _proj_a_kernel, "proj_a", (B, T // TM_A),
        [pl.BlockSpec((1, TM_A, D_MODEL), lambda b, i: (b, i, 0)),
         _const_spec((1, D_MODEL)), _const_spec(wq_t.shape), _const_spec(wn.shape),
         _const_spec(wr_t.shape), _const_spec((HEAD_DIM, 1)), _const_spec((1, LANES))],
        [x, norm_g.reshape(1, D_MODEL), wq_t, wn, wr_t, qn_g.reshape(HEAD_DIM, 1),
         jnp.concatenate([kn_g, kn_g]).reshape(1, LANES)],
        [o[0] for o in outs], [o[1](B) for o in outs], live_bytes=live)


def _proj_b_kernel(x_ref, gkv_ref, gb_ref, wk_ref, wv_ref, wq_ref, wg_ref, kg_ref, qg_ref,
                   qT_ref, k_ref, vT_ref, sg_ref, km_ref):
    x = x_ref[0]
    xr = x * _rms_rows(x)
    hkv = (xr * gkv_ref[...]).astype(BF16)
    hb = (xr * gb_ref[...]).astype(BF16)
    yk = _dot(hkv, wk_ref[...])
    means = []
    for m in range(2):
        kn = _half_norm(yk[:, m * LANES:(m + 1) * LANES], kg_ref[...])
        k_ref[0, m] = kn.astype(BF16)
        means.append(jnp.sum(kn, axis=0, keepdims=True) * (1.0 / MOBA_BLOCK))
    km_ref[0, 0] = jnp.concatenate(means, axis=1)
    _store_vT(vT_ref, _dot_t(wv_ref[...], hkv))
    _store_qT(qT_ref, _dot_t(wq_ref[...], hb), qg_ref[...])
    sg_ref[0] = _silu(_dot(hb, wg_ref[...])).astype(BF16)


def _proj_b(x, norm_kv_g, w_kv, kn_g, norm_b_g, w_in_b, qn_g):
    B, T, _ = x.shape
    assert T % MOBA_BLOCK == 0 and TQ == MOBA_BLOCK
    wk = w_kv[:, :KV_WIDTH].astype(BF16)
    wv_t = w_kv[:, KV_WIDTH:].T.astype(BF16)
    wq_t = w_in_b[:, :ATTN_WIDTH].T.astype(BF16)
    wg = w_in_b[:, ATTN_WIDTH:].astype(BF16)
    so = _proj_out_specs(TQ, T)
    outs = [so["qT"], so["k2"], so["vT"], so["sg"],
            (pl.BlockSpec((1, 1, 1, KV_WIDTH), lambda b, i: (b, i, 0, 0)),
             lambda B: jax.ShapeDtypeStruct((B, T // TQ, 1, KV_WIDTH), F32))]
    live = (_nbytes((TQ, D_MODEL), F32) + 2 * _nbytes((TQ, D_MODEL), BF16)
            + _nbytes((TQ, wk.shape[1] + wv_t.shape[0] + wq_t.shape[0] + wg.shape[1]), F32))
    return _call(
        _proj_b_kernel, "proj_b", (B, T // TQ),
        [pl.BlockSpec((1, TQ, D_MODEL), lambda b, i: (b, i, 0)),
         _const_spec((1, D_MODEL)), _const_spec((1, D_MODEL)),
         _const_spec(wk.shape), _const_spec(wv_t.shape), _const_spec(wq_t.shape),
         _const_spec(wg.shape), _const_spec((1, LANES)), _const_spec((HEAD_DIM, 1))],
        [x, norm_kv_g.reshape(1, D_MODEL), norm_b_g.reshape(1, D_MODEL), wk, wv_t, wq_t, wg,
         jnp.concatenate([kn_g, kn_g]).reshape(1, LANES), qn_g.reshape(HEAD_DIM, 1)],
        [o[0] for o in outs], [o[1](B) for o in outs], live_bytes=live)


def _two_per_body(n, fn):
    def body(j, carry):
        fn(2 * j, 2 * j + 1)
        return carry

    lax.fori_loop(0, lax.shift_right_logical(n, 1), body, 0)

    @pl.when((n & 1) == 1)
    def _():
        fn(n - 1)


def _attend(i, nt, qT_ref, k_ref, vT_ref, bias_ref, mask_fn, s_ref, m_ref, acc_ref):
    n_groups = N_HEADS // HB
    kv_per_group = HB // GROUP
    m_ref[...] = jnp.full(m_ref.shape, NEG_BIG, F32)
    acc_ref[...] = jnp.zeros(acc_ref.shape, F32)

    def slot(o, c):
        return c if o % 2 == 0 else jnp.where(c >= 2, c - 2, nt + c)

    def scores(o, c, bias_kind):
        koff = pl.multiple_of(c * TQ, TQ)
        base = slot(o, c) * HB
        for gi in range(kv_per_group):
            g = o * kv_per_group + gi
            kt = k_ref[0, g // 2, pl.ds(koff, TQ), :]
            mask = mask_fn(c, g, bias_kind == 1)
            for hl in range(GROUP * gi, GROUP * (gi + 1)):
                hd = o * HB + hl
                s = _dot(kt, qT_ref[0, hd])
                if bias_kind is not None:
                    s = s + bias_ref[bias_kind, hd]
                if mask is not None and mask[0] == "tile":
                    s = s + mask[1]
                s_ref[base + hl] = s
                m_cur = jnp.max(s, axis=0, keepdims=True)
                if mask is not None and mask[0] == "col":
                    m_cur = jnp.where(mask[1], m_cur, NEG_BIG)
                m_ref[hd] = jnp.maximum(m_ref[hd], m_cur)

    def values(o, c):
        base = slot(o, c) * HB
        for gi in range(kv_per_group):
            g = o * kv_per_group + gi
            mask = mask_fn(c, g, False)
            for pj in range(2):
                pair = 2 * g + pj
                pv = None
                for e in range(2):
                    hd = 2 * pair + e
                    m = m_ref[hd]
                    if mask is not None and mask[0] == "col":
                        m = jnp.where(mask[1], m, jnp.inf)
                    pr = jnp.exp2((s_ref[base + hd - o * HB] - m).astype(BF16))
                    d = _dot(vT_ref[0, 2 * g + e, c], pr)
                    pv = d if pv is None else pv + d
                acc_ref[pair] = acc_ref[pair] + pv

    far = jnp.maximum(i - 1, 0)
    for o in range(n_groups + 1):
        def far_chunks(*cs, o=o):
            for c in cs:
                if o >= 1:
                    values(o - 1, c)
            for c in cs:
                if o < n_groups:
                    scores(o, c, None)

        _two_per_body(far, far_chunks)

        @pl.when(i >= 1)
        def _(o=o):
            if o >= 1:
                values(o - 1, i - 1)
                values(o - 1, i)
            if o < n_groups:
                scores(o, i - 1, 0)
                scores(o, i, 1)

        @pl.when(i == 0)
        def _(o=o):
            if o >= 1:
                values(o - 1, i)
            if o < n_groups:
                scores(o, i, 1)


def _gated_out(acc_ref, sg_ref, x_ref, wout_ref, out_ref):
    og = []
    for p in range(N_PAIRS):
        acc = acc_ref[p]
        inv0 = 1.0 / acc[LANES:LANES + 1, :]
        inv1 = 1.0 / acc[LANES + SUBLANES:LANES + SUBLANES + 1, :]
        o_t = jnp.concatenate([acc[0:HALF, :] * inv0, acc[HALF:LANES, :] * inv1], axis=0)
        og.append((o_t.T * sg_ref[0, :, p * LANES:(p + 1) * LANES].astype(F32)).astype(BF16))
    out_ref[0] = x_ref[0] + _dot(jnp.concatenate(og, axis=1), wout_ref[...])


ATTN_LIVE_BYTES = (2 * HB * _nbytes((TQ, TQ), F32) + 2 * _nbytes((TQ, D_MODEL), F32)
                   + _nbytes((TQ, ATTN_WIDTH), BF16))


def _attn_scratch(nt):
    return [pltpu.VMEM(((nt + 2) * HB, TQ, TQ), F32),
            pltpu.VMEM((N_HEADS, 1, TQ), F32),
            pltpu.VMEM((N_PAIRS, VT_ROWS, TQ), F32)]


def _attn_specs(T):
    nt = T // TQ
    return dict(
        qT=pl.BlockSpec((1, N_HEADS, LANES, TQ), lambda b, i: (b, 0, 0, i)),
        k2=pl.BlockSpec((1, 2, T, LANES), lambda b, i: (b, 0, 0, 0)),
        vT=pl.BlockSpec((1, 2 * N_KV_HEADS, nt, VT_ROWS, TQ), lambda b, i: (b, 0, 0, 0, 0)),
        row=lambda n: pl.BlockSpec((1, TQ, n), lambda b, i: (b, i, 0)),
        wout=_const_spec((ATTN_WIDTH, D_MODEL)),
        bias=_const_spec((2, N_HEADS, TQ, TQ)),
    )


INT_MIN = -2 ** 31
KEY_POS_INF = 0x7F800000
KEY_NEG_INF = -KEY_POS_INF
KEY_LOW_BITS = 16


def _key_to_float(k):
    bits = jnp.where(k < 0, jnp.int32(INT_MIN) - k, k)
    return lax.bitcast_convert_type(bits, F32)


def _dsa_kernel(nt, topk, qT_ref, sg_ref, x_ref, iqT_ref, iwT_ref, k_ref, vT_ref, ik_ref,
                wout_ref, bias_ref, out_ref, sc16_ref, madd_ref, s_ref, m_ref, acc_ref):
    i = pl.program_id(1)
    key = lax.broadcasted_iota(I32, (TQ, TQ), 0)
    qry = lax.broadcasted_iota(I32, (TQ, TQ), 1)
    sub = lax.broadcasted_iota(I32, (SUBLANES, TQ), 0)
    assert nt <= 2 * HB
    sc_ref = s_ref.at[pl.ds(nt * HB, nt)]

    iw = iwT_ref[0]

    def score_chunk(c):
        ik = ik_ref[0, pl.ds(pl.multiple_of(c * TQ, TQ), TQ), :]
        acc = jnp.zeros((TQ, TQ), F32)
        for h in range(IDX_HEADS):
            acc = acc + iw[h:h + 1, :] * jnp.maximum(_dot(ik, iqT_ref[0, h]), 0.0)
        return acc

    def store_scores(c, s):
        sc_ref[c] = s
        sc16_ref[c] = s.astype(BF16)

    _two_per_body(i, lambda *cs: [store_scores(c, score_chunk(c)) for c in cs])
    store_scores(i, jnp.where(key > qry, -jnp.inf, score_chunk(i)))

    def count(pred):
        def body(c, tots):
            tots = list(tots)
            for r in range(TQ // SUBLANES):
                s = sc_ref[c, r * SUBLANES:(r + 1) * SUBLANES, :]
                hit = pred(s, c * TQ + r * SUBLANES + sub)
                tots[r % len(tots)] = tots[r % len(tots)] + jnp.where(hit, 1.0, 0.0)
            return tuple(tots)
        zero = jnp.zeros((SUBLANES, TQ), F32)
        t = lax.fori_loop(0, i + 1, body, (zero,) * 4)
        return jnp.sum((t[0] + t[1]) + (t[2] + t[3]), axis=0, keepdims=True)

    def count16(cand16):
        rows = 2 * SUBLANES
        one, zero = jnp.ones((rows, TQ), BF16), jnp.zeros((rows, TQ), BF16)

        def body(c, tots):
            tots = list(tots)
            for r in range(TQ // rows):
                s = sc16_ref[c, r * rows:(r + 1) * rows, :]
                tots[r % len(tots)] = tots[r % len(tots)] + jnp.where(s >= cand16, one, zero)
            return tuple(tots)
        assert nt * (TQ // rows) <= 256
        t = lax.fori_loop(0, i + 1, body, (zero,) * 4)
        return jnp.sum(((t[0] + t[1]) + (t[2] + t[3])).astype(F32), axis=0, keepdims=True)

    def coarse_body(b, u):
        uc = u | lax.shift_left(jnp.int32(1), 31 - b)
        in_range = (uc >= 0) | (uc <= jnp.int32(KEY_POS_INF - KEY_NEG_INF - 2 ** 32))
        cand = _key_to_float(jnp.int32(KEY_NEG_INF) + uc)
        cnt = count16(jnp.broadcast_to(cand.astype(BF16), (2 * SUBLANES, TQ)))
        return jnp.where(in_range & (cnt >= topk), uc, u)

    k16 = jnp.int32(KEY_NEG_INF) + lax.fori_loop(0, 32 - KEY_LOW_BITS, coarse_body,
                                                 jnp.zeros((1, TQ), I32))

    base = jnp.maximum(k16 - (1 << KEY_LOW_BITS), jnp.int32(KEY_NEG_INF))

    def fine_body(b, carry):
        u, cnt_t = carry
        uc = u | lax.shift_left(jnp.int32(1), KEY_LOW_BITS - b)
        cand = jnp.broadcast_to(_key_to_float(base + uc), (SUBLANES, TQ))
        cnt = count(lambda s, pos: s >= cand)
        ok = cnt >= topk
        return jnp.where(ok, uc, u), jnp.where(ok, cnt, cnt_t)

    cnt0 = jnp.full((1, TQ), float(nt * TQ), F32)
    u, cnt_t = lax.fori_loop(0, KEY_LOW_BITS + 1, fine_body, (jnp.zeros((1, TQ), I32), cnt0))
    thr = _key_to_float(base + u)

    excess = (cnt_t > topk) & (thr > -jnp.inf)
    any_excess = jnp.max(jnp.where(excess, 1.0, 0.0)) > 0.0

    @pl.when(jnp.logical_not(any_excess))
    def _():
        def body(c, carry):
            madd_ref[c] = jnp.where(sc_ref[c] >= thr, 0.0, -jnp.inf)
            return carry
        lax.fori_loop(0, i + 1, body, 0)

    @pl.when(any_excess)
    def _():
        thr8 = jnp.broadcast_to(thr, (SUBLANES, TQ))
        need = topk - count(lambda s, pos: s > thr8)

        pos_bits = (nt * TQ - 1).bit_length()

        def pos_body(b, qpos):
            cand = qpos | lax.shift_left(jnp.int32(1), pos_bits - 1 - b)
            f = count(lambda s, pos: (s == thr8) & (pos < cand))
            return jnp.where(f < need, cand, qpos)

        qpos = lax.fori_loop(0, pos_bits, pos_body, jnp.zeros((1, TQ), I32))
        qpos = jnp.where(excess, qpos, jnp.int32(nt * TQ))

        def body(c, carry):
            s = sc_ref[c]
            sel = (s > thr) | ((s == thr) & (c * TQ + key <= qpos))
            madd_ref[c] = jnp.where(sel, 0.0, -jnp.inf)
            return carry
        lax.fori_loop(0, i + 1, body, 0)

    _attend(i, nt, qT_ref, k_ref, vT_ref, bias_ref, lambda c, g, diag: ("tile", madd_ref[c]),
            s_ref, m_ref, acc_ref)
    _gated_out(acc_ref, sg_ref, x_ref, wout_ref, out_ref)


def _dsa_attention(x, qT, k2, vT, sg, iqT, iwT, ik, w_out, bias):
    B, T, _ = x.shape
    nt = T // TQ
    topk = min(DSA_TOPK_MAX, T // 4)
    sp = _attn_specs(T)
    return _call(
        functools.partial(_dsa_kernel, nt, topk), "dsa_attention", (B, nt),
        [sp["qT"], sp["row"](ATTN_WIDTH), sp["row"](D_MODEL),
         pl.BlockSpec((1, IDX_HEADS, LANES, TQ), lambda b, i: (b, 0, 0, i)),
         pl.BlockSpec((1, IDX_HEADS, TQ), lambda b, i: (b, 0, i)),
         sp["k2"], sp["vT"], pl.BlockSpec((1, T, LANES), lambda b, i: (b, 0, 0)),
         sp["wout"], sp["bias"]],
        [qT, sg, x, iqT, iwT, k2, vT, ik, w_out.astype(BF16), bias],
        [sp["row"](D_MODEL)], [jax.ShapeDtypeStruct((B, T, D_MODEL), F32)],
        scratch=[pltpu.VMEM((nt, TQ, TQ), BF16), pltpu.VMEM((nt, TQ, TQ), F32)] + _attn_scratch(nt),
        live_bytes=ATTN_LIVE_BYTES)[0]


def _split_bf16(a):
    hi = a.astype(BF16)
    return hi, (a - hi.astype(F32)).astype(BF16)


def _moba_kernel(nt, nsel, qT_ref, sg_ref, x_ref, k_ref, vT_ref, km_ref, wout_ref, bias_ref,
                 out_ref, sel_ref, s_ref, m_ref, acc_ref):
    i = pl.program_id(1)
    assert nt <= SUBLANES
    blk = lax.broadcasted_iota(I32, (SUBLANES, TQ), 0)

    km = jnp.concatenate([km_ref[0], jnp.zeros((2 * SUBLANES - nt, KV_WIDTH), F32)], axis=0)
    for g in range(N_KV_HEADS):
        qs = qT_ref[0, GROUP * g].astype(F32)
        for j in range(1, GROUP):
            qs = qs + qT_ref[0, GROUP * g + j].astype(F32)
        q_hi, q_lo = _split_bf16(qs)
        m_hi, m_lo = _split_bf16(km[:, (g // 2) * LANES:(g // 2 + 1) * LANES])
        gs = (_dot(m_hi, q_hi) + (_dot(m_hi, q_lo) + _dot(m_lo, q_hi)))[:SUBLANES]
        rank = jnp.zeros((SUBLANES, TQ), F32)
        for n in range(nt):
            gn = gs[n:n + 1, :]
            beats = (gn > gs) | ((gn == gs) & (n < blk))
            rank = rank + jnp.where(beats, jnp.where(n < i, 1.0, 0.0), 0.0)
        sel_ref[g] = jnp.where(((rank < nsel) & (blk < i)) | (blk == i), 1.0, 0.0)

    def mask_fn(c, g, diag):
        if diag:
            return None
        return ("col", sel_ref[g, pl.ds(c, 1), :] > 0.5)

    _attend(i, nt, qT_ref, k_ref, vT_ref, bias_ref, mask_fn, s_ref, m_ref, acc_ref)
    _gated_out(acc_ref, sg_ref, x_ref, wout_ref, out_ref)


def _moba_attention(x, qT, k2, vT, sg, kmean, w_out, bias):
    B, T, _ = x.shape
    nt = T // TQ
    nsel = min(MOBA_TOPK, nt - 1)
    sp = _attn_specs(T)
    return _call(
        functools.partial(_moba_kernel, nt, nsel), "moba_attention", (B, nt),
        [sp["qT"], sp["row"](ATTN_WIDTH), sp["row"](D_MODEL), sp["k2"], sp["vT"],
         pl.BlockSpec((1, nt, KV_WIDTH), lambda b, i: (b, 0, 0)), sp["wout"], sp["bias"]],
        [qT, sg, x, k2, vT, kmean, w_out.astype(BF16), bias],
        [sp["row"](D_MODEL)], [jax.ShapeDtypeStruct((B, T, D_MODEL), F32)],
        scratch=[pltpu.VMEM((N_KV_HEADS, SUBLANES, TQ), F32)] + _attn_scratch(nt),
        live_bytes=ATTN_LIVE_BYTES)[0]


def kernel(x, norm_a_g, w_in_a, qn_a_g, kn_a_g, w_out_a, rel_bias, norm_kv_g, w_kv, kn_b_g,
           norm_b_g, w_in_b, qn_b_g, w_out_b):
    B, T, _ = x.shape
    assert norm_a_g.shape[0] == 1 and norm_b_g.shape[0] == 1
    bias = _bias_tiles(rel_bias)
    qT, k2, vT, sg, iqT, iwT, ik = _proj_a(x, norm_a_g[0], w_in_a[0], qn_a_g[0], kn_a_g[0])
    h = _dsa_attention(x, qT, k2, vT, sg, iqT, iwT, ik, w_out_a[0], bias)
    qT, k2, vT, sg, kmean = _proj_b(h, norm_kv_g, w_kv, kn_b_g, norm_b_g[0], w_in_b[0], qn_b_g[0])
    return _moba_attention(h, qT, k2, vT, sg, kmean.reshape(B, T // TQ, KV_WIDTH), w_out_b[0], bias)
```

```python
import functools
import math

import jax
import jax.numpy as jnp
import numpy as np
from jax import lax
from jax.experimental import pallas as pl
from jax.experimental.pallas import tpu as pltpu

D_MODEL = 1024
N_HEADS = 16
HEAD_DIM = 64
N_KV_HEADS = 4
GROUP = 4
ATTN_WIDTH = 1024
KV_WIDTH = 256
IDX_HEADS = 8
IDX_DIM = 64
DSA_TOPK_MAX = 256
MOBA_BLOCK = 256
MOBA_TOPK = 3
REL_BUCKETS = 32
REL_MAX_DIST = 128
EPS = 1e-6

LANES = 128
SUBLANES = 8
HALF = LANES // 2
TQ = 256
TM_A = 1024
TM_B = 1024
N_PAIRS = N_HEADS // 2
HB = 8
VT_ROWS = LANES + 2 * SUBLANES
LOG2E = 1.4426950408889634
Q_SCALE = HEAD_DIM ** -0.5 * LOG2E
NEG_BIG = -1e30
V7X_VMEM_BYTES = 64 * 1024 * 1024

F32 = jnp.float32
BF16 = jnp.bfloat16
I32 = jnp.int32


def _bucket_starts():
    max_exact = REL_BUCKETS // 2
    n = np.arange(0, 2 * REL_MAX_DIST)
    nf = np.maximum(n, 1).astype(np.float64)
    large = max_exact + (np.log(nf / max_exact) / math.log(REL_MAX_DIST / max_exact)
                         * (REL_BUCKETS - max_exact)).astype(np.int64)
    bucket = np.where(n < max_exact, n, np.minimum(large, REL_BUCKETS - 1))
    nf32 = np.maximum(n, 1).astype(np.float32)
    large32 = max_exact + (np.log(nf32 / np.float32(max_exact)) / np.float32(math.log(REL_MAX_DIST / max_exact))
                           * np.float32(REL_BUCKETS - max_exact)).astype(np.int32)
    bucket32 = np.where(n < max_exact, n, np.minimum(large32, REL_BUCKETS - 1))
    assert (bucket == bucket32).all() and (np.diff(bucket) >= 0).all()
    return [int(np.argmax(bucket >= b)) for b in range(REL_BUCKETS)]


BUCKET_STARTS = _bucket_starts()


def _nbytes(shape, dtype):
    return math.prod(shape) * jnp.dtype(dtype).itemsize


def _cparams(n_axes, specs_and_dtypes, scratch=(), live_bytes=0):
    vmem = live_bytes + sum(_nbytes(s.shape, s.dtype) for s in scratch)
    for spec, dtype in specs_and_dtypes:
        if spec.block_shape is not None:
            buffers = spec.pipeline_mode.buffer_count if spec.pipeline_mode else 2
            vmem += buffers * _nbytes(spec.block_shape, dtype)
    assert vmem <= V7X_VMEM_BYTES, vmem
    return pltpu.CompilerParams(dimension_semantics=("arbitrary",) * n_axes,
                                vmem_limit_bytes=vmem)


def _call(kernel_fn, name, grid, in_specs, operands, out_specs, out_shapes, scratch=(), live_bytes=0):
    windows = [(s, a.dtype) for s, a in zip(in_specs, operands)]
    windows += [(s, o.dtype) for s, o in zip(out_specs, out_shapes)]
    return pl.pallas_call(
        kernel_fn, grid=grid, in_specs=in_specs, out_specs=out_specs, out_shape=out_shapes,
        scratch_shapes=list(scratch), name=name,
        compiler_params=_cparams(len(grid), windows, scratch, live_bytes),
    )(*operands)


def _dot_t(a, b):
    return lax.dot_general(a, b, (((1,), (1,)), ((), ())), preferred_element_type=F32)


def _dot(a, b):
    return jnp.dot(a, b, preferred_element_type=F32)


def _const_spec(shape):
    nd = len(shape)
    return pl.BlockSpec(shape, lambda *_: (0,) * nd, pipeline_mode=pl.Buffered(1))


def _bias_kernel(rb_ref, out_ref):
    kind = pl.program_id(0)
    h = pl.program_id(1)
    key = lax.broadcasted_iota(I32, (TQ, TQ), 0)
    qry = lax.broadcasted_iota(I32, (TQ, TQ), 1)
    n = qry - key + (1 - kind) * TQ
    last = rb_ref[REL_BUCKETS - 1, h]
    tile = jnp.full((TQ, TQ), last, F32)
    for b in range(REL_BUCKETS - 2, -1, -1):
        tile = jnp.where(n < BUCKET_STARTS[b + 1], rb_ref[b, h], tile)
    tile = (tile - last) * LOG2E
    out_ref[0, 0] = jnp.where(n < 0, -jnp.inf, tile)


def _bias_tiles(rel_bias):
    return _call(_bias_kernel, "bias_tiles", (2, N_HEADS),
                 [pl.BlockSpec(memory_space=pltpu.SMEM)], [rel_bias],
                 [pl.BlockSpec((1, 1, TQ, TQ), lambda k, h: (k, h, 0, 0))],
                 [jax.ShapeDtypeStruct((2, N_HEADS, TQ, TQ), F32)],
                 live_bytes=4 * _nbytes((TQ, TQ), F32))[0]


def _rms_rows(x):
    return lax.rsqrt(jnp.mean(x * x, axis=-1, keepdims=True) + EPS)


def _half_norm(t, gain2):
    lo = lax.broadcasted_iota(I32, t.shape, 1) < HALF
    s = t * t
    s_lo = jnp.sum(jnp.where(lo, s, 0.0), axis=1, keepdims=True)
    s_hi = jnp.sum(jnp.where(lo, 0.0, s), axis=1, keepdims=True)
    r_lo = lax.rsqrt(s_lo * (1.0 / HEAD_DIM) + EPS)
    r_hi = lax.rsqrt(s_hi * (1.0 / HEAD_DIM) + EPS)
    return t * jnp.where(lo, r_lo, r_hi) * gain2


def _in_half(slab, half):
    z = jnp.zeros_like(slab)
    return jnp.concatenate([slab, z] if half == 0 else [z, slab], axis=0)


def _store_qT(qT_ref, yqT, gain_col):
    for h in range(N_HEADS):
        slab = yqT[h * HEAD_DIM:(h + 1) * HEAD_DIM, :]
        r = lax.rsqrt(jnp.sum(slab * slab, axis=0, keepdims=True) * (1.0 / HEAD_DIM) + EPS)
        qn = slab * r * gain_col * Q_SCALE
        qT_ref[0, h] = _in_half(qn, (h // GROUP) % 2).astype(BF16)


def _store_vT(vT_ref, yvT):
    m = yvT.shape[1]
    ones = jnp.ones((SUBLANES, m), F32)
    zeros = jnp.zeros((SUBLANES, m), F32)
    for g in range(N_KV_HEADS):
        slab = yvT[g * HEAD_DIM:(g + 1) * HEAD_DIM, :]
        for e in range(2):
            tail = [ones, zeros] if e == 0 else [zeros, ones]
            tile = jnp.concatenate([_in_half(slab, e)] + tail, axis=0).astype(BF16)
            for cc in range(m // TQ):
                vT_ref[0, 2 * g + e, cc] = tile[:, cc * TQ:(cc + 1) * TQ]


def _silu(y):
    return y * jax.nn.sigmoid(y)


def _proj_out_specs(tm, T):
    nt = T // TQ
    specs = dict(
        qT=(pl.BlockSpec((1, N_HEADS, LANES, tm), lambda b, i: (b, 0, 0, i)),
            lambda B: jax.ShapeDtypeStruct((B, N_HEADS, LANES, T), BF16)),
        k2=(pl.BlockSpec((1, 2, tm, LANES), lambda b, i: (b, 0, i, 0)),
            lambda B: jax.ShapeDtypeStruct((B, 2, T, LANES), BF16)),
        vT=(pl.BlockSpec((1, 2 * N_KV_HEADS, tm // TQ, VT_ROWS, TQ), lambda b, i: (b, 0, i, 0, 0)),
            lambda B: jax.ShapeDtypeStruct((B, 2 * N_KV_HEADS, nt, VT_ROWS, TQ), BF16)),
        sg=(pl.BlockSpec((1, tm, ATTN_WIDTH), lambda b, i: (b, i, 0)),
            lambda B: jax.ShapeDtypeStruct((B, T, ATTN_WIDTH), BF16)),
    )
    return specs


def _proj_a_kernel(x_ref, g_ref, wq_ref, wn_ref, wr_ref, qg_ref, kg_ref,
                   qT_ref, k_ref, vT_ref, sg_ref, iqT_ref, iwT_ref, ik_ref):
    x = x_ref[0]
    h = (x * _rms_rows(x) * g_ref[...]).astype(BF16)
    _store_qT(qT_ref, _dot_t(wq_ref[...], h), qg_ref[...])
    yn = _dot(h, wn_ref[...])
    for m in range(2):
        k_ref[0, m] = _half_norm(yn[:, m * LANES:(m + 1) * LANES], kg_ref[...]).astype(BF16)
    sg_ref[0] = _silu(yn[:, KV_WIDTH:KV_WIDTH + ATTN_WIDTH]).astype(BF16)
    ik_ref[0] = yn[:, KV_WIDTH + ATTN_WIDTH:].astype(BF16)
    yr = _dot_t(wr_ref[...], h)
    _store_vT(vT_ref, yr[:KV_WIDTH])
    for hh in range(IDX_HEADS):
        slab = yr[KV_WIDTH + hh * IDX_DIM:KV_WIDTH + (hh + 1) * IDX_DIM, :]
        iqT_ref[0, hh] = _in_half(slab, 0).astype(BF16)
    iw0 = KV_WIDTH + IDX_HEADS * IDX_DIM
    iwT_ref[0] = yr[iw0:iw0 + IDX_HEADS, :] * (IDX_HEADS ** -0.5 * IDX_DIM ** -0.5)


def _proj_a(x, norm_g, w_in, qn_g, kn_g):
    B, T, _ = x.shape
    c = np.cumsum([0, ATTN_WIDTH, KV_WIDTH, KV_WIDTH, ATTN_WIDTH, IDX_HEADS * IDX_DIM, IDX_HEADS, IDX_DIM])
    part = lambda j: w_in[:, c[j]:c[j + 1]]
    wq_t = part(0).T.astype(BF16)
    wn = jnp.concatenate([part(1), part(3), jnp.pad(part(6), ((0, 0), (0, LANES - IDX_DIM)))],
                         axis=1).astype(BF16)
    wr_t = jnp.concatenate([part(2), part(4), jnp.pad(part(5), ((0, 0), (0, IDX_HEADS)))],
                           axis=1).T.astype(BF16)
    so = _proj_out_specs(TM_A, T)
    outs = [so["qT"], so["k2"], so["vT"], so["sg"],
            (pl.BlockSpec((1, IDX_HEADS, LANES, TM_A), lambda b, i: (b, 0, 0, i)),
             lambda B: jax.ShapeDtypeStruct((B, IDX_HEADS, LANES, T), BF16)),
            (pl.BlockSpec((1, IDX_HEADS, TM_A), lambda b, i: (b, 0, i)),
             lambda B: jax.ShapeDtypeStruct((B, IDX_HEADS, T), F32)),
            (pl.BlockSpec((1, TM_A, LANES), lambda b, i: (b, i, 0)),
             lambda B: jax.ShapeDtypeStruct((B, T, LANES), BF16))]
    live = (_nbytes((TM_A, D_MODEL), F32) + _nbytes((TM_A, D_MODEL), BF16)
            + _nbytes((TM_A, wq_t.shape[0] + wn.shape[1] + wr_t.shape[0]), F32))
    return _call(
        _proj_a_kernel, "proj_a", (B, T // TM_A),
        [pl.BlockSpec((1, TM_A, D_MODEL), lambda b, i: (b, i, 0)),
         _const_spec((1, D_MODEL)), _const_spec(wq_t.shape), _const_spec(wn.shape),
         _const_spec(wr_t.shape), _const_spec((HEAD_DIM, 1)), _const_spec((1, LANES))],
        [x, norm_g.reshape(1, D_MODEL), wq_t, wn, wr_t, qn_g.reshape(HEAD_DIM, 1),
         jnp.concatenate([kn_g, kn_g]).reshape(1, LANES)],
        [o[0] for o in outs], [o[1](B) for o in outs], live_bytes=live)


def _proj_b_kernel(x_ref, gkv_ref, gb_ref, wk_ref, wv_ref, wq_ref, wg_ref, kg_ref, qg_ref,
                   qT_ref, k_ref, vT_ref, sg_ref, km_ref):
    x = x_ref[0]
    xr = x * _rms_rows(x)
    hkv = (xr * gkv_ref[...]).astype(BF16)
    hb = (xr * gb_ref[...]).astype(BF16)
    yk = _dot(hkv, wk_ref[...])
    kn = [_half_norm(yk[:, m * LANES:(m + 1) * LANES], kg_ref[...]) for m in range(2)]
    for m in range(2):
        k_ref[0, m] = kn[m].astype(BF16)
    for blk in range(TM_B // MOBA_BLOCK):
        rows = slice(blk * MOBA_BLOCK, (blk + 1) * MOBA_BLOCK)
        km_ref[0, blk] = jnp.concatenate(
            [jnp.sum(t[rows], axis=0, keepdims=True) * (1.0 / MOBA_BLOCK) for t in kn], axis=1)
    _store_vT(vT_ref, _dot_t(wv_ref[...], hkv))
    _store_qT(qT_ref, _dot_t(wq_ref[...], hb), qg_ref[...])
    sg_ref[0] = _silu(_dot(hb, wg_ref[...])).astype(BF16)


def _proj_b(x, norm_kv_g, w_kv, kn_g, norm_b_g, w_in_b, qn_g):
    B, T, _ = x.shape
    assert T % TM_B == 0 and TM_B % MOBA_BLOCK == 0 and TQ == MOBA_BLOCK
    wk = w_kv[:, :KV_WIDTH].astype(BF16)
    wv_t = w_kv[:, KV_WIDTH:].T.astype(BF16)
    wq_t = w_in_b[:, :ATTN_WIDTH].T.astype(BF16)
    wg = w_in_b[:, ATTN_WIDTH:].astype(BF16)
    so = _proj_out_specs(TM_B, T)
    outs = [so["qT"], so["k2"], so["vT"], so["sg"],
            (pl.BlockSpec((1, TM_B // TQ, 1, KV_WIDTH), lambda b, i: (b, i, 0, 0)),
             lambda B: jax.ShapeDtypeStruct((B, T // TQ, 1, KV_WIDTH), F32))]
    live = (_nbytes((TM_B, D_MODEL), F32) + 2 * _nbytes((TM_B, D_MODEL), BF16)
            + _nbytes((TM_B, wk.shape[1] + wv_t.shape[0] + wq_t.shape[0] + wg.shape[1]), F32))
    return _call(
        _proj_b_kernel, "proj_b", (B, T // TM_B),
        [pl.BlockSpec((1, TM_B, D_MODEL), lambda b, i: (b, i, 0)),
         _const_spec((1, D_MODEL)), _const_spec((1, D_MODEL)),
         _const_spec(wk.shape), _const_spec(wv_t.shape), _const_spec(wq_t.shape),
         _const_spec(wg.shape), _const_spec((1, LANES)), _const_spec((HEAD_DIM, 1))],
        [x, norm_kv_g.reshape(1, D_MODEL), norm_b_g.reshape(1, D_MODEL), wk, wv_t, wq_t, wg,
         jnp.concatenate([kn_g, kn_g]).reshape(1, LANES), qn_g.reshape(HEAD_DIM, 1)],
        [o[0] for o in outs], [o[1](B) for o in outs], live_bytes=live)


def _chunks_per_body(n, fn):
    quads = lax.shift_right_logical(n, 2)

    def body(j, carry):
        fn(4 * j, 4 * j + 1, 4 * j + 2, 4 * j + 3)
        return carry

    lax.fori_loop(0, quads, body, 0)

    @pl.when((n & 2) != 0)
    def _():
        fn(4 * quads, 4 * quads + 1)

    @pl.when((n & 1) == 1)
    def _():
        fn(n - 1)


def _attend(i, nt, qT_ref, k_ref, vT_ref, bias_ref, mask_fn, s_ref, m_ref, acc_ref):
    n_groups = N_HEADS // HB
    kv_per_group = HB // GROUP
    m_ref[...] = jnp.full(m_ref.shape, NEG_BIG, F32)
    acc_ref[...] = jnp.zeros(acc_ref.shape, F32)

    def slot(o, c):
        return c if o % 2 == 0 else jnp.where(c >= 2, c - 2, nt + c)

    def scores(o, c, bias_kind):
        koff = pl.multiple_of(c * TQ, TQ)
        base = slot(o, c) * HB
        for gi in range(kv_per_group):
            g = o * kv_per_group + gi
            kt = k_ref[0, g // 2, pl.ds(koff, TQ), :]
            mask = mask_fn(c, g, bias_kind == 1)
            for hl in range(GROUP * gi, GROUP * (gi + 1)):
                hd = o * HB + hl
                s = _dot(kt, qT_ref[0, hd])
                if bias_kind is not None:
                    s = s + bias_ref[bias_kind, hd]
                if mask is not None and mask[0] == "tile":
                    s = s + mask[1]
                s_ref[base + hl] = s
                m_cur = jnp.max(s, axis=0, keepdims=True)
                if mask is not None and mask[0] == "col":
                    m_cur = jnp.where(mask[1], m_cur, NEG_BIG)
                m_ref[hd] = jnp.maximum(m_ref[hd], m_cur)

    def values(o, c):
        base = slot(o, c) * HB
        for gi in range(kv_per_group):
            g = o * kv_per_group + gi
            mask = mask_fn(c, g, False)
            for pj in range(2):
                pair = 2 * g + pj
                pv = None
                for e in range(2):
                    hd = 2 * pair + e
                    m = m_ref[hd]
                    if mask is not None and mask[0] == "col":
                        m = jnp.where(mask[1], m, jnp.inf)
                    pr = jnp.exp2((s_ref[base + hd - o * HB] - m).astype(BF16))
                    d = _dot(vT_ref[0, 2 * g + e, c], pr)
                    pv = d if pv is None else pv + d
                acc_ref[pair] = acc_ref[pair] + pv

    far = jnp.maximum(i - 1, 0)
    for o in range(n_groups + 1):
        def far_chunks(*cs, o=o):
            for c in cs:
                if o >= 1:
                    values(o - 1, c)
            for c in cs:
                if o < n_groups:
                    scores(o, c, None)

        _chunks_per_body(far, far_chunks)

        @pl.when(i >= 1)
        def _(o=o):
            if o >= 1:
                values(o - 1, i - 1)
                values(o - 1, i)
            if o < n_groups:
                scores(o, i - 1, 0)
                scores(o, i, 1)

        @pl.when(i == 0)
        def _(o=o):
            if o >= 1:
                values(o - 1, i)
            if o < n_groups:
                scores(o, i, 1)


def _gated_out(acc_ref, sg_ref, x_ref, wout_ref, out_ref):
    og = []
    for p in range(N_PAIRS):
        acc = acc_ref[p]
        inv0 = 1.0 / acc[LANES:LANES + 1, :]
        inv1 = 1.0 / acc[LANES + SUBLANES:LANES + SUBLANES + 1, :]
        o_t = jnp.concatenate([acc[0:HALF, :] * inv0, acc[HALF:LANES, :] * inv1], axis=0)
        og.append((o_t.T * sg_ref[0, :, p * LANES:(p + 1) * LANES].astype(F32)).astype(BF16))
    out_ref[0] = x_ref[0] + _dot(jnp.concatenate(og, axis=1), wout_ref[...])


ATTN_LIVE_BYTES = (2 * HB * _nbytes((TQ, TQ), F32) + 2 * _nbytes((TQ, D_MODEL), F32)
                   + _nbytes((TQ, ATTN_WIDTH), BF16))


def _attn_scratch(nt):
    return [pltpu.VMEM(((nt + 2) * HB, TQ, TQ), F32),
            pltpu.VMEM((N_HEADS, 1, TQ), F32),
            pltpu.VMEM((N_PAIRS, VT_ROWS, TQ), F32)]


def _attn_specs(T):
    nt = T // TQ
    return dict(
        qT=pl.BlockSpec((1, N_HEADS, LANES, TQ), lambda b, i: (b, 0, 0, i)),
        k2=pl.BlockSpec((1, 2, T, LANES), lambda b, i: (b, 0, 0, 0)),
        vT=pl.BlockSpec((1, 2 * N_KV_HEADS, nt, VT_ROWS, TQ), lambda b, i: (b, 0, 0, 0, 0)),
        row=lambda n: pl.BlockSpec((1, TQ, n), lambda b, i: (b, i, 0)),
        wout=_const_spec((ATTN_WIDTH, D_MODEL)),
        bias=_const_spec((2, N_HEADS, TQ, TQ)),
    )


INT_MIN = -2 ** 31
KEY_POS_INF = 0x7F800000
KEY_NEG_INF = -KEY_POS_INF
KEY_LOW_BITS = 16


def _key_to_float(k):
    bits = jnp.where(k < 0, jnp.int32(INT_MIN) - k, k)
    return lax.bitcast_convert_type(bits, F32)


def _dsa_kernel(nt, topk, qT_ref, sg_ref, x_ref, iqT_ref, iwT_ref, k_ref, vT_ref, ik_ref,
                wout_ref, bias_ref, out_ref, sc16_ref, madd_ref, s_ref, m_ref, acc_ref):
    i = pl.program_id(1)
    key = lax.broadcasted_iota(I32, (TQ, TQ), 0)
    qry = lax.broadcasted_iota(I32, (TQ, TQ), 1)
    sub = lax.broadcasted_iota(I32, (SUBLANES, TQ), 0)
    assert nt <= 2 * HB
    sc_ref = s_ref.at[pl.ds(nt * HB, nt)]

    iw = iwT_ref[0]

    def score_chunk(c):
        ik = ik_ref[0, pl.ds(pl.multiple_of(c * TQ, TQ), TQ), :]
        acc = jnp.zeros((TQ, TQ), F32)
        for h in range(IDX_HEADS):
            acc = acc + iw[h:h + 1, :] * jnp.maximum(_dot(ik, iqT_ref[0, h]), 0.0)
        return acc

    def store_scores(c, s):
        sc_ref[c] = s
        sc16_ref[c] = s.astype(BF16)

    _chunks_per_body(i, lambda *cs: [store_scores(c, score_chunk(c)) for c in cs])
    store_scores(i, jnp.where(key > qry, -jnp.inf, score_chunk(i)))

    def over_chunks(body, init):
        n = i + 1
        carry = lax.fori_loop(0, lax.shift_right_logical(n, 1),
                              lambda j, c: body(2 * j + 1, body(2 * j, c)), init)
        return lax.cond((n & 1) == 1, lambda c: body(n - 1, c), lambda c: c, carry)

    def count(pred):
        def body(c, tots):
            tots = list(tots)
            for r in range(TQ // SUBLANES):
                s = sc_ref[c, r * SUBLANES:(r + 1) * SUBLANES, :]
                hit = pred(s, c * TQ + r * SUBLANES + sub)
                tots[r % len(tots)] = tots[r % len(tots)] + jnp.where(hit, 1.0, 0.0)
            return tuple(tots)
        zero = jnp.zeros((SUBLANES, TQ), F32)
        t = over_chunks(body, (zero,) * 4)
        return jnp.sum((t[0] + t[1]) + (t[2] + t[3]), axis=0, keepdims=True)

    def count16(cand16):
        rows = 2 * SUBLANES
        one, zero = jnp.ones((rows, TQ), BF16), jnp.zeros((rows, TQ), BF16)

        def body(c, tots):
            tots = list(tots)
            for r in range(TQ // rows):
                s = sc16_ref[c, r * rows:(r + 1) * rows, :]
                tots[r % len(tots)] = tots[r % len(tots)] + jnp.where(s >= cand16, one, zero)
            return tuple(tots)
        assert nt * (TQ // rows) <= 256
        t = over_chunks(body, (zero,) * 4)
        return jnp.sum(((t[0] + t[1]) + (t[2] + t[3])).astype(F32), axis=0, keepdims=True)

    def coarse_body(b, u):
        uc = u | lax.shift_left(jnp.int32(1), 31 - b)
        in_range = (uc >= 0) | (uc <= jnp.int32(KEY_POS_INF - KEY_NEG_INF - 2 ** 32))
        cand = _key_to_float(jnp.int32(KEY_NEG_INF) + uc)
        cnt = count16(jnp.broadcast_to(cand.astype(BF16), (2 * SUBLANES, TQ)))
        return jnp.where(in_range & (cnt >= topk), uc, u)

    k16 = jnp.int32(KEY_NEG_INF) + lax.fori_loop(0, 32 - KEY_LOW_BITS, coarse_body,
                                                 jnp.zeros((1, TQ), I32))

    base = jnp.maximum(k16 - (1 << KEY_LOW_BITS), jnp.int32(KEY_NEG_INF))

    def fine_body(b, carry):
        u, cnt_t = carry
        uc = u | lax.shift_left(jnp.int32(1), KEY_LOW_BITS - b)
        cand = jnp.broadcast_to(_key_to_float(base + uc), (SUBLANES, TQ))
        cnt = count(lambda s, pos: s >= cand)
        ok = cnt >= topk
        return jnp.where(ok, uc, u), jnp.where(ok, cnt, cnt_t)

    cnt0 = jnp.full((1, TQ), float(nt * TQ), F32)
    u, cnt_t = lax.fori_loop(0, KEY_LOW_BITS + 1, fine_body, (jnp.zeros((1, TQ), I32), cnt0))
    thr = _key_to_float(base + u)

    excess = (cnt_t > topk) & (thr > -jnp.inf)
    any_excess = jnp.max(jnp.where(excess, 1.0, 0.0)) > 0.0

    @pl.when(jnp.logical_not(any_excess))
    def _():
        def body(c, carry):
            madd_ref[c] = jnp.where(sc_ref[c] >= thr, 0.0, -jnp.inf)
            return carry
        lax.fori_loop(0, i + 1, body, 0)

    @pl.when(any_excess)
    def _():
        thr8 = jnp.broadcast_to(thr, (SUBLANES, TQ))
        need = topk - count(lambda s, pos: s > thr8)

        pos_bits = (nt * TQ - 1).bit_length()

        def pos_body(b, qpos):
            cand = qpos | lax.shift_left(jnp.int32(1), pos_bits - 1 - b)
            f = count(lambda s, pos: (s == thr8) & (pos < cand))
            return jnp.where(f < need, cand, qpos)

        qpos = lax.fori_loop(0, pos_bits, pos_body, jnp.zeros((1, TQ), I32))
        qpos = jnp.where(excess, qpos, jnp.int32(nt * TQ))

        def body(c, carry):
            s = sc_ref[c]
            sel = (s > thr) | ((s == thr) & (c * TQ + key <= qpos))
            madd_ref[c] = jnp.where(sel, 0.0, -jnp.inf)
            return carry
        lax.fori_loop(0, i + 1, body, 0)

    _attend(i, nt, qT_ref, k_ref, vT_ref, bias_ref, lambda c, g, diag: ("tile", madd_ref[c]),
            s_ref, m_ref, acc_ref)
    _gated_out(acc_ref, sg_ref, x_ref, wout_ref, out_ref)


def _dsa_attention(x, qT, k2, vT, sg, iqT, iwT, ik, w_out, bias):
    B, T, _ = x.shape
    nt = T // TQ
    topk = min(DSA_TOPK_MAX, T // 4)
    sp = _attn_specs(T)
    return _call(
        functools.partial(_dsa_kernel, nt, topk), "dsa_attention", (B, nt),
        [sp["qT"], sp["row"](ATTN_WIDTH), sp["row"](D_MODEL),
         pl.BlockSpec((1, IDX_HEADS, LANES, TQ), lambda b, i: (b, 0, 0, i)),
         pl.BlockSpec((1, IDX_HEADS, TQ), lambda b, i: (b, 0, i)),
         sp["k2"], sp["vT"], pl.BlockSpec((1, T, LANES), lambda b, i: (b, 0, 0)),
         sp["wout"], sp["bias"]],
        [qT, sg, x, iqT, iwT, k2, vT, ik, w_out.astype(BF16), bias],
        [sp["row"](D_MODEL)], [jax.ShapeDtypeStruct((B, T, D_MODEL), F32)],
        scratch=[pltpu.VMEM((nt, TQ, TQ), BF16), pltpu.VMEM((nt, TQ, TQ), F32)] + _attn_scratch(nt),
        live_bytes=ATTN_LIVE_BYTES)[0]


def _split_bf16(a):
    hi = a.astype(BF16)
    return hi, (a - hi.astype(F32)).astype(BF16)


def _moba_kernel(nt, nsel, qT_ref, sg_ref, x_ref, k_ref, vT_ref, km_ref, wout_ref, bias_ref,
                 out_ref, sel_ref, s_ref, m_ref, acc_ref):
    i = pl.program_id(1)
    assert nt <= SUBLANES
    blk = lax.broadcasted_iota(I32, (SUBLANES, TQ), 0)

    km = jnp.concatenate([km_ref[0], jnp.zeros((2 * SUBLANES - nt, KV_WIDTH), F32)], axis=0)
    for g in range(N_KV_HEADS):
        qs = qT_ref[0, GROUP * g].astype(F32)
        for j in range(1, GROUP):
            qs = qs + qT_ref[0, GROUP * g + j].astype(F32)
        q_hi, q_lo = _split_bf16(qs)
        m_hi, m_lo = _split_bf16(km[:, (g // 2) * LANES:(g // 2 + 1) * LANES])
        gs = (_dot(m_hi, q_hi) + (_dot(m_hi, q_lo) + _dot(m_lo, q_hi)))[:SUBLANES]
        rank = jnp.zeros((SUBLANES, TQ), F32)
        for n in range(nt):
            gn = gs[n:n + 1, :]
            beats = (gn > gs) | ((gn == gs) & (n < blk))
            rank = rank + jnp.where(beats, jnp.where(n < i, 1.0, 0.0), 0.0)
        sel_ref[g] = jnp.where(((rank < nsel) & (blk < i)) | (blk == i), 1.0, 0.0)

    def mask_fn(c, g, diag):
        if diag:
            return None
        return ("col", sel_ref[g, pl.ds(c, 1), :] > 0.5)

    _attend(i, nt, qT_ref, k_ref, vT_ref, bias_ref, mask_fn, s_ref, m_ref, acc_ref)
    _gated_out(acc_ref, sg_ref, x_ref, wout_ref, out_ref)


def _moba_attention(x, qT, k2, vT, sg, kmean, w_out, bias):
    B, T, _ = x.shape
    nt = T // TQ
    nsel = min(MOBA_TOPK, nt - 1)
    sp = _attn_specs(T)
    return _call(
        functools.partial(_moba_kernel, nt, nsel), "moba_attention", (B, nt),
        [sp["qT"], sp["row"](ATTN_WIDTH), sp["row"](D_MODEL), sp["k2"], sp["vT"],
         pl.BlockSpec((1, nt, KV_WIDTH), lambda b, i: (b, 0, 0)), sp["wout"], sp["bias"]],
        [qT, sg, x, k2, vT, kmean, w_out.astype(BF16), bias],
        [sp["row"](D_MODEL)], [jax.ShapeDtypeStruct((B, T, D_MODEL), F32)],
        scratch=[pltpu.VMEM((N_KV_HEADS, SUBLANES, TQ), F32)] + _attn_scratch(nt),
        live_bytes=ATTN_LIVE_BYTES)[0]


def kernel(x, norm_a_g, w_in_a, qn_a_g, kn_a_g, w_out_a, rel_bias, norm_kv_g, w_kv, kn_b_g,
           norm_b_g, w_in_b, qn_b_g, w_out_b):
    B, T, _ = x.shape
    assert norm_a_g.shape[0] == 1 and norm_b_g.shape[0] == 1
    bias = _bias_tiles(rel_bias)
    qT, k2, vT, sg, iqT, iwT, ik = _proj_a(x, norm_a_g[0], w_in_a[0], qn_a_g[0], kn_a_g[0])
    h = _dsa_attention(x, qT, k2, vT, sg, iqT, iwT, ik, w_out_a[0], bias)
    qT, k2, vT, sg, kmean = _proj_b(h, norm_kv_g, w_kv, kn_b_g, norm_b_g[0], w_in_b[0], qn_b_g[0])
    return _moba_attention(h, qT, k2, vT, sg, kmean.reshape(B, T // TQ, KV_WIDTH), w_out_b[0], bias)
```
